```python
import math
import jax, jax.numpy as jnp
from jax import lax
import numpy as np

D_MODEL = 2048
BATCH = 4
SEQ = 4096
DEPTH = 2

NSA_HEADS = 16
NSA_KV_GROUPS = 4
NSA_HEAD_DIM = 64
NSA_CMP_LEN = 32
NSA_CMP_STRIDE = 16
NSA_CMP_HIDDEN = 256
NSA_SEL_LEN = 64
NSA_SEL_BLOCKS = 16
NSA_WINDOW = 512
DSA_HEADS = 16
DSA_HEAD_DIM = 64
DSA_KV_LATENT = 256
IDX_HEADS = 16
IDX_DIM = 32
DSA_TOPK_MAX = 256
INDEX_SCALE = (IDX_HEADS * IDX_DIM) ** -0.5
D_FF = 5632
REL_BUCKETS = 32
REL_MAX_DIST = 128
TOTAL_HEADS = NSA_HEADS + DSA_HEADS
Q_BLOCK = 128
N_SUBLAYERS = 3
ADA_INIT = 0.5
RMS_EPS = 1e-6
NEG_INF = -1e30
FORCE_BONUS = 1e4

NSA_KV_WIDTH = NSA_KV_GROUPS * NSA_HEAD_DIM
IN_SPLITS = (
    NSA_HEADS * NSA_HEAD_DIM,
    NSA_KV_WIDTH, NSA_KV_WIDTH,
    NSA_KV_WIDTH, NSA_KV_WIDTH,
    NSA_KV_WIDTH, NSA_KV_WIDTH,
    NSA_HEADS * 3,
    DSA_HEADS * DSA_HEAD_DIM,
    DSA_KV_LATENT,
    IDX_HEADS * IDX_DIM,
    IDX_DIM,
    IDX_HEADS,
    2 * D_MODEL,
)
N_IN = sum(IN_SPLITS)

kernel_name = "hybrid_nsa_dsa_macaron_adaln"


def _rmsnorm(x, g):
    xf = x.astype(jnp.float32)
    y = xf * lax.rsqrt(jnp.mean(xf * xf, axis=-1, keepdims=True) + RMS_EPS)
    return (y * g.astype(jnp.float32)).astype(x.dtype)


def _masked_softmax(s, mask):
    s = jnp.where(mask, s.astype(jnp.float32), NEG_INF)
    m = jnp.max(s, axis=-1, keepdims=True)
    p = jnp.where(mask, jnp.exp(s - m), 0.0)
    return p / jnp.maximum(jnp.sum(p, axis=-1, keepdims=True), 1e-30)


def _t5_bucket(dist):
    n = jnp.maximum(dist, 0)
    exact = REL_BUCKETS // 2
    nf = jnp.maximum(n, 1).astype(jnp.float32)
    large = exact + (jnp.log(nf / exact) / math.log(REL_MAX_DIST / exact)
                     * (REL_BUCKETS - exact)).astype(jnp.int32)
    large = jnp.minimum(large, REL_BUCKETS - 1)
    return jnp.where(n < exact, n, large)


def _to_blocks(a):
    b, s = a.shape[:2]
    return jnp.moveaxis(a.reshape(b, s // Q_BLOCK, Q_BLOCK, *a.shape[2:]), 1, 0)


def _from_blocks(o):
    nq, b, qb, f = o.shape
    return jnp.moveaxis(o, 0, 1).reshape(b, nq * qb, f)


def _swiglu(h, w_i, w_o):
    g, u = jnp.split(h @ w_i, 2, axis=-1)
    return (jax.nn.silu(g) * u) @ w_o


def _modulate(x, g, m):
    return _rmsnorm(x, g) * (1.0 + m[:, 1][:, None, :]) + m[:, 0][:, None, :]


def _nsa(q, kc, vc, ks, vs, kw, vw, gates, pe_k, pe_v, wk1, wk2, wv1, wv2, rel_bias):
    B, S = q.shape[:2]
    H, G, DK = NSA_HEADS, NSA_KV_GROUPS, NSA_HEAD_DIM
    HG = H // G
    scale = DK ** -0.5
    f32 = jnp.float32

    n_cmp = (S - NSA_CMP_LEN) // NSA_CMP_STRIDE + 1
    cmp_start = (np.arange(n_cmp) * NSA_CMP_STRIDE).astype(np.int32)
    blk_idx = cmp_start[:, None] + np.arange(NSA_CMP_LEN, dtype=np.int32)[None, :]
    cmp_end = jnp.asarray(cmp_start + NSA_CMP_LEN - 1)

    def compress(t, pe, w1, w2):
        blocks = jnp.moveaxis(t[:, blk_idx], 3, 2) + pe
        flat = blocks.reshape(B, n_cmp, G, NSA_CMP_LEN * DK)
        return jax.nn.silu(flat @ w1) @ w2

    k_cmp = compress(kc, pe_k, wk1, wk2)
    v_cmp = compress(vc, pe_v, wv1, wv2)

    n_blk = S // NSA_SEL_LEN
    n_sel = min(NSA_SEL_BLOCKS, n_blk)
    sel_start = np.arange(n_blk) * NSA_SEL_LEN
    overlap = jnp.asarray(((cmp_start[:, None] < sel_start[None, :] + NSA_SEL_LEN)
                           & (cmp_start[:, None] + NSA_CMP_LEN > sel_start[None, :])).astype(np.float32))
    k_sb = ks.reshape(B, n_blk, NSA_SEL_LEN, G, DK).transpose(0, 3, 1, 2, 4)
    v_sb = vs.reshape(B, n_blk, NSA_SEL_LEN, G, DK).transpose(0, 3, 1, 2, 4)

    pad = ((0, 0), (NSA_WINDOW, 0), (0, 0), (0, 0))
    kw_pad = jnp.pad(kw, pad)
    vw_pad = jnp.pad(vw, pad)
    table_n = rel_bias[:, :H]
    table_g = table_n.reshape(REL_BUCKETS, G, HG)
    wq = jnp.arange(Q_BLOCK)
    wk = jnp.arange(Q_BLOCK + NSA_WINDOW)
    off_w = wq[:, None] - wk[None, :] + NSA_WINDOW
    bias_w = table_n[_t5_bucket(off_w)].reshape(Q_BLOCK, Q_BLOCK + NSA_WINDOW, G, HG).transpose(0, 2, 3, 1)
    mask_w_rel = (off_w >= 0) & (off_w < NSA_WINDOW)
    bidx = jnp.arange(B)[:, None, None, None]
    gidx = jnp.arange(G)[None, None, :, None]
    blk = jnp.arange(n_blk)

    def block(args):
        i, qb, gb = args
        t = i * Q_BLOCK + wq
        qg = qb.reshape(B, Q_BLOCK, G, HG, DK)
        gg = gb.reshape(B, Q_BLOCK, G, HG, 3)

        dist_c = t[:, None] - cmp_end[None, :]
        bias_c = table_n[_t5_bucket(dist_c)].reshape(Q_BLOCK, n_cmp, G, HG).transpose(0, 2, 3, 1)
        s_c = jnp.einsum('btghd,bngd->btghn', qg, k_cmp).astype(f32) * scale + bias_c
        p_c = _masked_softmax(s_c, (dist_c >= 0)[:, None, None, :])
        o_c = jnp.einsum('btghn,bngd->btghd', p_c.astype(v_cmp.dtype), v_cmp)

        imp = jnp.einsum('btghn,nj->btgj', p_c, overlap)
        cur = t // NSA_SEL_LEN
        forced = (blk[None, :] == 0) | (blk[None, :] == cur[:, None]) | (blk[None, :] == cur[:, None] - 1)
        visible = blk[None, :] * NSA_SEL_LEN <= t[:, None]
        imp = jnp.where(visible[None, :, None, :],
                        imp + jnp.where(forced, FORCE_BONUS, 0.0)[None, :, None, :], NEG_INF)
        _, sel = lax.top_k(imp, n_sel)
        n_tok = n_sel * NSA_SEL_LEN
        k_g = k_sb[bidx, gidx, sel].reshape(B, Q_BLOCK, G, n_tok, DK)
        v_g = v_sb[bidx, gidx, sel].reshape(B, Q_BLOCK, G, n_tok, DK)
        pos = (sel[..., None] * NSA_SEL_LEN + jnp.arange(NSA_SEL_LEN)).reshape(B, Q_BLOCK, G, n_tok)
        dist_s = t[None, :, None, None] - pos
        bias_s = jnp.moveaxis(table_g[_t5_bucket(dist_s), gidx], -1, 3)
        s_s = jnp.einsum('btghd,btgkd->btghk', qg, k_g).astype(f32) * scale + bias_s
        p_s = _masked_softmax(s_s, (dist_s >= 0)[:, :, :, None, :])
        o_s = jnp.einsum('btghk,btgkd->btghd', p_s.astype(v_g.dtype), v_g)

        k_wb = lax.dynamic_slice_in_dim(kw_pad, i * Q_BLOCK, Q_BLOCK + NSA_WINDOW, axis=1)
        v_wb = lax.dynamic_slice_in_dim(vw_pad, i * Q_BLOCK, Q_BLOCK + NSA_WINDOW, axis=1)
        mask_w = mask_w_rel & ((i * Q_BLOCK - NSA_WINDOW + wk) >= 0)[None, :]
        s_w = jnp.einsum('btghd,bsgd->btghs', qg, k_wb).astype(f32) * scale + bias_w
        p_w = _masked_softmax(s_w, mask_w[:, None, None, :])
        o_w = jnp.einsum('btghs,bsgd->btghd', p_w.astype(v_wb.dtype), v_wb)

        o = gg[..., 0:1] * o_c + gg[..., 1:2] * o_s + gg[..., 2:3] * o_w
        return o.reshape(B, Q_BLOCK, H * DK)

    nq = S // Q_BLOCK
    out = lax.map(block, (jnp.arange(nq), _to_blocks(q), _to_blocks(gates)))
    return _from_blocks(out)


def _dsa(q, ckv, qi, ki, wi, w_uk, w_uv, rel_bias):
    B, S = q.shape[:2]
    k_top = min(DSA_TOPK_MAX, S // 4)
    scale = DSA_HEAD_DIM ** -0.5
    table_d = rel_bias[:, NSA_HEADS:]
    bidx = jnp.arange(B)[:, None, None]
    s_pos = jnp.arange(S)

    def block(args):
        i, qb, qib, wib = args
        t = i * Q_BLOCK + jnp.arange(Q_BLOCK)
        idx_logits = jnp.einsum('bthe,bse->bths', qib, ki)
        score = jnp.einsum('bths,bth->bts', jax.nn.relu(idx_logits), wib).astype(jnp.float32)
        score = jnp.where(s_pos[None, None, :] <= t[None, :, None], score, NEG_INF)
        _, idx = lax.top_k(score, k_top)
        c_g = ckv[bidx, idx]
        q_lat = jnp.einsum('bthd,chd->bthc', qb, w_uk)
        dist = t[None, :, None] - idx
        bias = jnp.moveaxis(table_d[_t5_bucket(dist)], -1, 2)
        s = jnp.einsum('bthc,btkc->bthk', q_lat, c_g).astype(jnp.float32) * scale + bias
        p = _masked_softmax(s, (dist >= 0)[:, :, None, :])
        o_lat = jnp.einsum('bthk,btkc->bthc', p.astype(c_g.dtype), c_g)
        o = jnp.einsum('bthc,chd->bthd', o_lat, w_uv)
        return o.reshape(B, Q_BLOCK, DSA_HEADS * DSA_HEAD_DIM)

    nq = S // Q_BLOCK
    out = lax.map(block, (jnp.arange(nq), _to_blocks(q), _to_blocks(qi), _to_blocks(wi)))
    return _from_blocks(out)


def _mixer(h, w_in, pe_k, pe_v, wk1, wk2, wv1, wv2, g_kv, w_uk, w_uv,
           w_up_nsa, w_up_dsa, w_out, rel_bias):
    B, S, D = h.shape
    cuts = [int(v) for v in np.cumsum(IN_SPLITS)[:-1]]
    (q_n, kc, vc, ks, vs, kw, vw, g_n, q_d, ckv, qi, ki, wi, a_m) = jnp.split(h @ w_in, cuts, axis=-1)
    kv_shape = (B, S, NSA_KV_GROUPS, NSA_HEAD_DIM)
    o_n = _nsa(q_n.reshape(B, S, NSA_HEADS, NSA_HEAD_DIM),
               kc.reshape(kv_shape), vc.reshape(kv_shape),
               ks.reshape(kv_shape), vs.reshape(kv_shape),
               kw.reshape(kv_shape), vw.reshape(kv_shape),
               jax.nn.sigmoid(g_n).reshape(B, S, NSA_HEADS, 3),
               pe_k, pe_v, wk1, wk2, wv1, wv2, rel_bias)
    o_d = _dsa(q_d.reshape(B, S, DSA_HEADS, DSA_HEAD_DIM),
               _rmsnorm(ckv, g_kv),
               qi.reshape(B, S, IDX_HEADS, IDX_DIM), ki, wi * INDEX_SCALE,
               w_uk, w_uv, rel_bias)
    a_m = jax.nn.sigmoid(a_m).reshape(B, S, 2, D)
    y = a_m[:, :, 0] * (o_n @ w_up_nsa) + a_m[:, :, 1] * (o_d @ w_up_dsa)
    return y @ w_out


def setup_inputs(seed: int = 0) -> dict:
    key = jax.random.key(seed)
    k = jax.random.split(key, 24)
    f32 = jnp.float32
    D = D_MODEL

    def nrm(kk, shape, s):
        return jax.random.normal(kk, shape, f32) * s

    cmp_in = NSA_CMP_LEN * NSA_HEAD_DIM
    return {
        "x": nrm(k[0], (BATCH, SEQ, D), 1.0),
        "c": nrm(k[1], (BATCH, D), 1.0),
        "w_ada": nrm(k[2], (DEPTH, D, N_SUBLAYERS * 3 * D), ADA_INIT * D ** -0.5),
        "b_ada": nrm(k[3], (DEPTH, N_SUBLAYERS * 3 * D), 0.02),
        "g_norm": 1.0 + nrm(k[4], (DEPTH, N_SUBLAYERS, D), 0.02),
        "w_ffn_in": nrm(k[5], (DEPTH, 2, D, 2 * D_FF), D ** -0.5),
        "w_ffn_out": nrm(k[6], (DEPTH, 2, D_FF, D), D_FF ** -0.5),
        "w_in": nrm(k[7], (DEPTH, D, N_IN), D ** -0.5),
        "nsa_pe_k": nrm(k[8], (DEPTH, NSA_CMP_LEN, NSA_HEAD_DIM), 0.5),
        "nsa_pe_v": nrm(k[9], (DEPTH, NSA_CMP_LEN, NSA_HEAD_DIM), 0.5),
        "nsa_cmp_k1": nrm(k[10], (DEPTH, cmp_in, NSA_CMP_HIDDEN), cmp_in ** -0.5),
        "nsa_cmp_k2": nrm(k[11], (DEPTH, NSA_CMP_HIDDEN, NSA_HEAD_DIM), NSA_CMP_HIDDEN ** -0.5),
        "nsa_cmp_v1": nrm(k[12], (DEPTH, cmp_in, NSA_CMP_HIDDEN), cmp_in ** -0.5),
        "nsa_cmp_v2": nrm(k[13], (DEPTH, NSA_CMP_HIDDEN, NSA_HEAD_DIM), NSA_CMP_HIDDEN ** -0.5),
        "dsa_g_kv": 1.0 + nrm(k[14], (DEPTH, DSA_KV_LATENT), 0.02),
        "dsa_w_uk": nrm(k[15], (DEPTH, DSA_KV_LATENT, DSA_HEADS, DSA_HEAD_DIM), DSA_KV_LATENT ** -0.5),
        "dsa_w_uv": nrm(k[16], (DEPTH, DSA_KV_LATENT, DSA_HEADS, DSA_HEAD_DIM), DSA_KV_LATENT ** -0.5),
        "w_up_nsa": nrm(k[17], (DEPTH, NSA_HEADS * NSA_HEAD_DIM, D), (NSA_HEADS * NSA_HEAD_DIM) ** -0.5),
        "w_up_dsa": nrm(k[18], (DEPTH, DSA_HEADS * DSA_HEAD_DIM, D), (DSA_HEADS * DSA_HEAD_DIM) ** -0.5),
        "w_out": nrm(k[19], (DEPTH, D, D), D ** -0.5),
        "rel_bias": nrm(k[20], (REL_BUCKETS, TOTAL_HEADS), 0.5),
        "g_final": 1.0 + nrm(k[21], (D,), 0.02),
    }


def reference(x, c, w_ada, b_ada, g_norm, w_ffn_in, w_ffn_out, w_in,
              nsa_pe_k, nsa_pe_v, nsa_cmp_k1, nsa_cmp_k2, nsa_cmp_v1, nsa_cmp_v2,
              dsa_g_kv, dsa_w_uk, dsa_w_uv, w_up_nsa, w_up_dsa, w_out, rel_bias, g_final):
    B = x.shape[0]
    c_act = jax.nn.silu(c)
    for l in range(DEPTH):
        mod = (c_act @ w_ada[l] + b_ada[l]).reshape(B, N_SUBLAYERS, 3, D_MODEL)
        h = _modulate(x, g_norm[l, 0], mod[:, 0])
        x = x + 0.5 * mod[:, 0, 2][:, None, :] * _swiglu(h, w_ffn_in[l, 0], w_ffn_out[l, 0])
        h = _modulate(x, g_norm[l, 1], mod[:, 1])
        y = _mixer(h, w_in[l], nsa_pe_k[l], nsa_pe_v[l], nsa_cmp_k1[l], nsa_cmp_k2[l],
                   nsa_cmp_v1[l], nsa_cmp_v2[l], dsa_g_kv[l], dsa_w_uk[l], dsa_w_uv[l],
                   w_up_nsa[l], w_up_dsa[l], w_out[l], rel_bias)
        x = x + mod[:, 1, 2][:, None, :] * y
        h = _modulate(x, g_norm[l, 2], mod[:, 2])
        x = x + 0.5 * mod[:, 2, 2][:, None, :] * _swiglu(h, w_ffn_in[l, 1], w_ffn_out[l, 1])
    return _rmsnorm(x, g_final)
```

```python
import functools
import math

import jax
import jax.numpy as jnp
import numpy as np
from jax import lax
from jax.experimental import pallas as pl
from jax.experimental.pallas import tpu as pltpu

D_MODEL = 2048
DEPTH = 2
NSA_HEADS = 16
NSA_KV_GROUPS = 4
NSA_HEAD_DIM = 64
NSA_CMP_LEN = 32
NSA_CMP_STRIDE = 16
NSA_CMP_HIDDEN = 256
NSA_SEL_LEN = 64
NSA_SEL_BLOCKS = 16
NSA_WINDOW = 512
DSA_HEADS = 16
DSA_HEAD_DIM = 64
DSA_KV_LATENT = 256
IDX_HEADS = 16
IDX_DIM = 32
DSA_TOPK_MAX = 256
INDEX_SCALE = (IDX_HEADS * IDX_DIM) ** -0.5
D_FF = 5632
REL_BUCKETS = 32
REL_MAX_DIST = 128
Q_BLOCK = 128
N_SUBLAYERS = 3
RMS_EPS = 1e-6
NEG_INF = -1e30
FORCE_BONUS = 1e4
NSA_KV_WIDTH = NSA_KV_GROUPS * NSA_HEAD_DIM

MXU_DTYPE = jnp.bfloat16
VMEM_LIMIT_BYTES = 56 * 1024 * 1024

COL_AM = 0
COL_QN = 4096
COL_QD = 5120
COL_KC, COL_VC, COL_KS, COL_VS, COL_KW, COL_VW = 6144, 6400, 6656, 6912, 7168, 7424
COL_QI = 7680
COL_CKV = 8192
COL_SMALL = 8448
N_IN_PAD = 8704


def _dot(a, b):
    return jnp.dot(a, b, preferred_element_type=jnp.float32)


def _rms_scale(x):
    return x * lax.rsqrt(jnp.mean(x * x, axis=-1, keepdims=True) + RMS_EPS)


def _modulated_norm(x, g, shift, scale):
    return (_rms_scale(x) * g) * (1.0 + scale) + shift


def _adaln_kernel(c_ref, w_ref, b_ref, o_ref):
    c = c_ref[...]
    c_act = c * jax.nn.sigmoid(c)
    o_ref[0] = jnp.dot(c_act, w_ref[0], preferred_element_type=jnp.float32,
                       precision=lax.Precision.HIGHEST) + b_ref[0]


def _adaln(c, w_ada, b_ada):
    depth, d, n = w_ada.shape
    b = c.shape[0]
    tn = 1024
    return pl.pallas_call(
        _adaln_kernel,
        grid=(depth, n // tn),
        in_specs=[
            pl.BlockSpec((b, d), lambda l, j: (0, 0)),
            pl.BlockSpec((1, d, tn), lambda l, j: (l, 0, j)),
            pl.BlockSpec((1, 1, tn), lambda l, j: (l, 0, j)),
        ],
        out_specs=pl.BlockSpec((1, b, tn), lambda l, j: (l, 0, j)),
        out_shape=jax.ShapeDtypeStruct((depth, b, n), jnp.float32),
        compiler_params=pltpu.CompilerParams(
            dimension_semantics=("arbitrary", "arbitrary"), vmem_limit_bytes=VMEM_LIMIT_BYTES),
        name="adaln",
    )(c, w_ada, b_ada.reshape(depth, 1, n))


def _ffn_kernel(x_ref, shift_ref, scale_ref, gate_ref, g_ref, wg_ref, wu_ref, wo_ref, gfin_ref, o_ref,
                h_scr, acc_scr, *, final_norm):
    j = pl.program_id(1)

    @pl.when(j == 0)
    def _():
        h = _modulated_norm(x_ref[...], g_ref[...], shift_ref[0], scale_ref[0])
        h_scr[...] = h.astype(h_scr.dtype)
        acc_scr[...] = jnp.zeros_like(acc_scr)

    h = h_scr[...]
    g = _dot(h, wg_ref[...])
    u = _dot(h, wu_ref[...])
    a = (g * jax.nn.sigmoid(g)) * u
    acc_scr[...] += _dot(a.astype(wo_ref.dtype), wo_ref[...])

    @pl.when(j == pl.num_programs(1) - 1)
    def _():
        y = x_ref[...] + (0.5 * gate_ref[0]) * acc_scr[...]
        if final_norm:
            y = _rms_scale(y) * gfin_ref[...]
        o_ref[...] = y


def _ffn(x2d, shift, scale, gate, g, w_i, w_o, g_final, *, seq, final_norm):
    n, d = x2d.shape
    tm, tf = 512, 512
    nf = D_FF // tf
    per_b = seq // tm
    mod_spec = pl.BlockSpec((1, 1, d), lambda i, j: (i // per_b, 0, 0))
    return pl.pallas_call(
        functools.partial(_ffn_kernel, final_norm=final_norm),
        grid=(n // tm, nf),
        in_specs=[
            pl.BlockSpec((tm, d), lambda i, j: (i, 0)),
            mod_spec, mod_spec, mod_spec,
            pl.BlockSpec((1, d), lambda i, j: (0, 0)),
            pl.BlockSpec((d, tf), lambda i, j: (0, j)),
            pl.BlockSpec((d, tf), lambda i, j: (0, j + nf)),
            pl.BlockSpec((tf, d), lambda i, j: (j, 0)),
            pl.BlockSpec((1, d), lambda i, j: (0, 0)),
        ],
        out_specs=pl.BlockSpec((tm, d), lambda i, j: (i, 0)),
        out_shape=jax.ShapeDtypeStruct((n, d), jnp.float32),
        scratch_shapes=[pltpu.VMEM((tm, d), MXU_DTYPE), pltpu.VMEM((tm, d), jnp.float32)],
        compiler_params=pltpu.CompilerParams(
            dimension_semantics=("arbitrary", "arbitrary"), vmem_limit_bytes=VMEM_LIMIT_BYTES),
        name="ffn",
    )(x2d, shift, scale, gate, g.reshape(1, d), w_i, w_i, w_o, g_final.reshape(1, d))


def _inproj_kernel(x_ref, shift_ref, scale_ref, g_ref, w_ref, o_ref, h_scr):
    @pl.when(pl.program_id(1) == 0)
    def _():
        h = _modulated_norm(x_ref[...], g_ref[...], shift_ref[0], scale_ref[0])
        h_scr[...] = h.astype(h_scr.dtype)

    o_ref[...] = _dot(h_scr[...], w_ref[...]).astype(o_ref.dtype)


def _inproj(x2d, shift, scale, g, w, *, seq):
    n, d = x2d.shape
    tm, tn = 512, 512
    per_b = seq // tm
    mod_spec = pl.BlockSpec((1, 1, d), lambda i, j: (i // per_b, 0, 0))
    return pl.pallas_call(
        _inproj_kernel,
        grid=(n // tm, N_IN_PAD // tn),
        in_specs=[
            pl.BlockSpec((tm, d), lambda i, j: (i, 0)),
            mod_spec, mod_spec,
            pl.BlockSpec((1, d), lambda i, j: (0, 0)),
            pl.BlockSpec((d, tn), lambda i, j: (0, j)),
        ],
        out_specs=pl.BlockSpec((tm, tn), lambda i, j: (i, j)),
        out_shape=jax.ShapeDtypeStruct((n, N_IN_PAD), MXU_DTYPE),
        scratch_shapes=[pltpu.VMEM((tm, d), MXU_DTYPE)],
        compiler_params=pltpu.CompilerParams(
            dimension_semantics=("arbitrary", "arbitrary"), vmem_limit_bytes=VMEM_LIMIT_BYTES),
        name="inproj",
    )(x2d, shift, scale, g.reshape(1, d), w)


def _merge_kernel(x_ref, gate_ref, on_ref, od_ref, a0_ref, a1_ref, wun_ref, wud_ref, wout_ref, o_ref):
    yn = _dot(on_ref[...], wun_ref[...])
    yd = _dot(od_ref[...], wud_ref[...])
    a0 = jax.nn.sigmoid(a0_ref[...].astype(jnp.float32))
    a1 = jax.nn.sigmoid(a1_ref[...].astype(jnp.float32))
    y = a0 * yn + a1 * yd
    o_ref[...] = x_ref[...] + gate_ref[0] * _dot(y.astype(wout_ref.dtype), wout_ref[...])


def _merge(x2d, gate, o_n, o_d, proj, w_up_nsa, w_up_dsa, w_out, *, seq):
    n, d = x2d.shape
    tm = 512
    per_b = seq // tm
    hw = o_n.shape[1]
    const = dict(pipeline_mode=pl.Buffered(1))
    return pl.pallas_call(
        _merge_kernel,
        grid=(n // tm,),
        in_specs=[
            pl.BlockSpec((tm, d), lambda i: (i, 0)),
            pl.BlockSpec((1, 1, d), lambda i: (i // per_b, 0, 0)),
            pl.BlockSpec((tm, hw), lambda i: (i, 0)),
            pl.BlockSpec((tm, hw), lambda i: (i, 0)),
            pl.BlockSpec((tm, d), lambda i: (i, COL_AM // d)),
            pl.BlockSpec((tm, d), lambda i: (i, COL_AM // d + 1)),
            pl.BlockSpec((hw, d), lambda i: (0, 0), **const),
            pl.BlockSpec((hw, d), lambda i: (0, 0), **const),
            pl.BlockSpec((d, d), lambda i: (0, 0), **const),
        ],
        out_specs=pl.BlockSpec((tm, d), lambda i: (i, 0)),
        out_shape=jax.ShapeDtypeStruct((n, d), jnp.float32),
        compiler_params=pltpu.CompilerParams(
            dimension_semantics=("arbitrary",), vmem_limit_bytes=VMEM_LIMIT_BYTES),
        name="merge",
    )(x2d, gate, o_n, o_d, proj, proj, w_up_nsa, w_up_dsa, w_out)


def _masked_softmax(s, mask):
    s = jnp.where(mask, s.astype(jnp.float32), NEG_INF)
    m = jnp.max(s, axis=-1, keepdims=True)
    p = jnp.where(mask, jnp.exp(s - m), 0.0)
    return p / jnp.maximum(jnp.sum(p, axis=-1, keepdims=True), 1e-30)


def _t5_bucket(dist):
    n = jnp.maximum(dist, 0)
    exact = REL_BUCKETS // 2
    nf = jnp.maximum(n, 1).astype(jnp.float32)
    large = exact + (jnp.log(nf / exact) / math.log(REL_MAX_DIST / exact)
                     * (REL_BUCKETS - exact)).astype(jnp.int32)
    large = jnp.minimum(large, REL_BUCKETS - 1)
    return jnp.where(n < exact, n, large)


def _to_blocks(a):
    b, s = a.shape[:2]
    return jnp.moveaxis(a.reshape(b, s // Q_BLOCK, Q_BLOCK, *a.shape[2:]), 1, 0)


def _from_blocks(o):
    nq, b, qb, f = o.shape
    return jnp.moveaxis(o, 0, 1).reshape(b, nq * qb, f)


def _nsa_jax(q, kc, vc, ks, vs, kw, vw, gates, pe_k, pe_v, wk1, wk2, wv1, wv2, rel_bias):
    B, S = q.shape[:2]
    H, G, DK = NSA_HEADS, NSA_KV_GROUPS, NSA_HEAD_DIM
    HG = H // G
    scale = DK ** -0.5
    f32 = jnp.float32
    n_cmp = (S - NSA_CMP_LEN) // NSA_CMP_STRIDE + 1
    cmp_start = (np.arange(n_cmp) * NSA_CMP_STRIDE).astype(np.int32)
    blk_idx = cmp_start[:, None] + np.arange(NSA_CMP_LEN, dtype=np.int32)[None, :]
    cmp_end = jnp.asarray(cmp_start + NSA_CMP_LEN - 1)

    def compress(t, pe, w1, w2):
        blocks = jnp.moveaxis(t[:, blk_idx], 3, 2) + pe
        flat = blocks.reshape(B, n_cmp, G, NSA_CMP_LEN * DK)
        return jax.nn.silu(flat @ w1) @ w2

    k_cmp = compress(kc, pe_k, wk1, wk2)
    v_cmp = compress(vc, pe_v, wv1, wv2)
    n_blk = S // NSA_SEL_LEN
    n_sel = min(NSA_SEL_BLOCKS, n_blk)
    sel_start = np.arange(n_blk) * NSA_SEL_LEN
    overlap = jnp.asarray(((cmp_start[:, None] < sel_start[None, :] + NSA_SEL_LEN)
                           & (cmp_start[:, None] + NSA_CMP_LEN > sel_start[None, :])).astype(np.float32))
    k_sb = ks.reshape(B, n_blk, NSA_SEL_LEN, G, DK).transpose(0, 3, 1, 2, 4)
    v_sb = vs.reshape(B, n_blk, NSA_SEL_LEN, G, DK).transpose(0, 3, 1, 2, 4)
    pad = ((0, 0), (NSA_WINDOW, 0), (0, 0), (0, 0))
    kw_pad = jnp.pad(kw, pad)
    vw_pad = jnp.pad(vw, pad)
    table_n = rel_bias[:, :H]
    table_g = table_n.reshape(REL_BUCKETS, G, HG)
    wq = jnp.arange(Q_BLOCK)
    wk = jnp.arange(Q_BLOCK + NSA_WINDOW)
    off_w = wq[:, None] - wk[None, :] + NSA_WINDOW
    bias_w = table_n[_t5_bucket(off_w)].reshape(Q_BLOCK, Q_BLOCK + NSA_WINDOW, G, HG).transpose(0, 2, 3, 1)
    mask_w_rel = (off_w >= 0) & (off_w < NSA_WINDOW)
    bidx = jnp.arange(B)[:, None, None, None]
    gidx = jnp.arange(G)[None, None, :, None]
    blk = jnp.arange(n_blk)

    def block(args):
        i, qb, gb = args
        t = i * Q_BLOCK + wq
        qg = qb.reshape(B, Q_BLOCK, G, HG, DK)
        gg = gb.reshape(B, Q_BLOCK, G, HG, 3)
        dist_c = t[:, None] - cmp_end[None, :]
        bias_c = table_n[_t5_bucket(dist_c)].reshape(Q_BLOCK, n_cmp, G, HG).transpose(0, 2, 3, 1)
        s_c = jnp.einsum('btghd,bngd->btghn', qg, k_cmp).astype(f32) * scale + bias_c
        p_c = _masked_softmax(s_c, (dist_c >= 0)[:, None, None, :])
        o_c = jnp.einsum('btghn,bngd->btghd', p_c.astype(v_cmp.dtype), v_cmp)
        imp = jnp.einsum('btghn,nj->btgj', p_c, overlap)
        cur = t // NSA_SEL_LEN
        forced = (blk[None, :] == 0) | (blk[None, :] == cur[:, None]) | (blk[None, :] == cur[:, None] - 1)
        visible = blk[None, :] * NSA_SEL_LEN <= t[:, None]
        imp = jnp.where(visible[None, :, None, :],
                        imp + jnp.where(forced, FORCE_BONUS, 0.0)[None, :, None, :], NEG_INF)
        _, sel = lax.top_k(imp, n_sel)
        n_tok = n_sel * NSA_SEL_LEN
        k_g = k_sb[bidx, gidx, sel].reshape(B, Q_BLOCK, G, n_tok, DK)
        v_g = v_sb[bidx, gidx, sel].reshape(B, Q_BLOCK, G, n_tok, DK)
        pos = (sel[..., None] * NSA_SEL_LEN + jnp.arange(NSA_SEL_LEN)).reshape(B, Q_BLOCK, G, n_tok)
        dist_s = t[None, :, None, None] - pos
        bias_s = jnp.moveaxis(table_g[_t5_bucket(dist_s), gidx], -1, 3)
        s_s = jnp.einsum('btghd,btgkd->btghk', qg, k_g).astype(f32) * scale + bias_s
        p_s = _masked_softmax(s_s, (dist_s >= 0)[:, :, :, None, :])
        o_s = jnp.einsum('btghk,btgkd->btghd', p_s.astype(v_g.dtype), v_g)
        k_wb = lax.dynamic_slice_in_dim(kw_pad, i * Q_BLOCK, Q_BLOCK + NSA_WINDOW, axis=1)
        v_wb = lax.dynamic_slice_in_dim(vw_pad, i * Q_BLOCK, Q_BLOCK + NSA_WINDOW, axis=1)
        mask_w = mask_w_rel & ((i * Q_BLOCK - NSA_WINDOW + wk) >= 0)[None, :]
        s_w = jnp.einsum('btghd,bsgd->btghs', qg, k_wb).astype(f32) * scale + bias_w
        p_w = _masked_softmax(s_w, mask_w[:, None, None, :])
        o_w = jnp.einsum('btghs,bsgd->btghd', p_w.astype(v_wb.dtype), v_wb)
        o = gg[..., 0:1] * o_c + gg[..., 1:2] * o_s + gg[..., 2:3] * o_w
        return o.reshape(B, Q_BLOCK, H * DK)

    nq = S // Q_BLOCK
    out = lax.map(block, (jnp.arange(nq), _to_blocks(q), _to_blocks(gates)))
    return _from_blocks(out)


def _dsa_jax(q, ckv, qi, ki, wi, w_uk, w_uv, rel_bias):
    B, S = q.shape[:2]
    k_top = min(DSA_TOPK_MAX, S // 4)
    scale = DSA_HEAD_DIM ** -0.5
    table_d = rel_bias[:, NSA_HEADS:]
    bidx = jnp.arange(B)[:, None, None]
    s_pos = jnp.arange(S)

    def block(args):
        i, qb, qib, wib = args
        t = i * Q_BLOCK + jnp.arange(Q_BLOCK)
        idx_logits = jnp.einsum('bthe,bse->bths', qib, ki)
        score = jnp.einsum('bths,bth->bts', jax.nn.relu(idx_logits), wib).astype(jnp.float32)
        score = jnp.where(s_pos[None, None, :] <= t[None, :, None], score, NEG_INF)
        _, idx = lax.top_k(score, k_top)
        c_g = ckv[bidx, idx]
        q_lat = jnp.einsum('bthd,chd->bthc', qb, w_uk)
        dist = t[None, :, None] - idx
        bias = jnp.moveaxis(table_d[_t5_bucket(dist)], -1, 2)
        s = jnp.einsum('bthc,btkc->bthk', q_lat, c_g).astype(jnp.float32) * scale + bias
        p = _masked_softmax(s, (dist >= 0)[:, :, None, :])
        o_lat = jnp.einsum('bthk,btkc->bthc', p.astype(c_g.dtype), c_g)
        o = jnp.einsum('bthc,chd->bthd', o_lat, w_uv)
        return o.reshape(B, Q_BLOCK, DSA_HEADS * DSA_HEAD_DIM)

    nq = S // Q_BLOCK
    out = lax.map(block, (jnp.arange(nq), _to_blocks(q), _to_blocks(qi), _to_blocks(wi)))
    return _from_blocks(out)


def _permute_w_in(w_in_l):
    d = w_in_l.shape[0]
    sizes = (1024, 256, 256, 256, 256, 256, 256, 48, 1024, 256, 512, 32, 16, 4096)
    offs = np.concatenate([[0], np.cumsum(sizes)])
    (q_n, kc, vc, ks, vs, kw, vw, g_n, q_d, ckv, qi, ki, wi, a_m) = [
        w_in_l[:, int(offs[k]):int(offs[k + 1])] for k in range(len(sizes))]
    small = jnp.concatenate([g_n, ki, wi, jnp.zeros((d, 32), w_in_l.dtype)], axis=1)
    out = jnp.concatenate([a_m, q_n, q_d, kc, vc, ks, vs, kw, vw, qi, ckv, small,
                           jnp.zeros((d, N_IN_PAD - 8576), w_in_l.dtype)], axis=1)
    return out.astype(MXU_DTYPE)


def kernel(x, c, w_ada, b_ada, g_norm, w_ffn_in, w_ffn_out, w_in, nsa_pe_k, nsa_pe_v, nsa_cmp_k1, nsa_cmp_k2,
           nsa_cmp_v1, nsa_cmp_v2, dsa_g_kv, dsa_w_uk, dsa_w_uv, w_up_nsa, w_up_dsa, w_out, rel_bias, g_final):
    B, S, D = x.shape
    N = B * S
    f32 = jnp.float32
    mod = _adaln(c, w_ada, b_ada).reshape(DEPTH, B, N_SUBLAYERS, 3, 1, D)
    x2d = x.reshape(N, D)
    for l in range(DEPTH):
        last = l == DEPTH - 1
        m0, m1, m2 = mod[l, :, 0], mod[l, :, 1], mod[l, :, 2]
        x2d = _ffn(x2d, m0[:, 0], m0[:, 1], m0[:, 2], g_norm[l, 0],
                   w_ffn_in[l, 0].astype(MXU_DTYPE), w_ffn_out[l, 0].astype(MXU_DTYPE), g_final,
                   seq=S, final_norm=False)
        proj = _inproj(x2d, m1[:, 0], m1[:, 1], g_norm[l, 1], _permute_w_in(w_in[l]), seq=S)
        p3 = proj.reshape(B, S, N_IN_PAD).astype(f32)

        def piece(off, width):
            return p3[:, :, off:off + width]

        kv_shape = (B, S, NSA_KV_GROUPS, NSA_HEAD_DIM)
        small = piece(COL_SMALL, 128)
        g_n, ki, wi = small[..., :48], small[..., 48:80], small[..., 80:96]
        o_n = _nsa_jax(piece(COL_QN, 1024).reshape(B, S, NSA_HEADS, NSA_HEAD_DIM),
                       piece(COL_KC, 256).reshape(kv_shape), piece(COL_VC, 256).reshape(kv_shape),
                       piece(COL_KS, 256).reshape(kv_shape), piece(COL_VS, 256).reshape(kv_shape),
                       piece(COL_KW, 256).reshape(kv_shape), piece(COL_VW, 256).reshape(kv_shape),
                       jax.nn.sigmoid(g_n).reshape(B, S, NSA_HEADS, 3),
                       nsa_pe_k[l], nsa_pe_v[l], nsa_cmp_k1[l], nsa_cmp_k2[l], nsa_cmp_v1[l], nsa_cmp_v2[l],
                       rel_bias)
        ckv = piece(COL_CKV, 256)
        ckv = _rms_scale(ckv) * dsa_g_kv[l]
        o_d = _dsa_jax(piece(COL_QD, 1024).reshape(B, S, DSA_HEADS, DSA_HEAD_DIM), ckv,
                       piece(COL_QI, 512).reshape(B, S, IDX_HEADS, IDX_DIM), ki, wi * INDEX_SCALE,
                       dsa_w_uk[l], dsa_w_uv[l], rel_bias)
        x2d = _merge(x2d, m1[:, 2], o_n.reshape(N, -1).astype(MXU_DTYPE), o_d.reshape(N, -1).astype(MXU_DTYPE),
                     proj, w_up_nsa[l].astype(MXU_DTYPE), w_up_dsa[l].astype(MXU_DTYPE),
                     w_out[l].astype(MXU_DTYPE), seq=S)
        x2d = _ffn(x2d, m2[:, 0], m2[:, 1], m2[:, 2], g_norm[l, 2],
                   w_ffn_in[l, 1].astype(MXU_DTYPE), w_ffn_out[l, 1].astype(MXU_DTYPE), g_final,
                   seq=S, final_norm=last)
    return x2d.reshape(B, S, D)
```

```python
import functools
import math

import jax
import jax.numpy as jnp
import numpy as np
from jax import lax
from jax.experimental import pallas as pl
from jax.experimental.pallas import tpu as pltpu

D_MODEL = 2048
DEPTH = 2
NSA_HEADS = 16
NSA_KV_GROUPS = 4
NSA_HEAD_DIM = 64
NSA_CMP_LEN = 32
NSA_CMP_STRIDE = 16
NSA_CMP_HIDDEN = 256
NSA_SEL_LEN = 64
NSA_SEL_BLOCKS = 16
NSA_WINDOW = 512
DSA_HEADS = 16
DSA_HEAD_DIM = 64
DSA_KV_LATENT = 256
IDX_HEADS = 16
IDX_DIM = 32
DSA_TOPK_MAX = 256
INDEX_SCALE = (IDX_HEADS * IDX_DIM) ** -0.5
D_FF = 5632
REL_BUCKETS = 32
REL_MAX_DIST = 128
Q_BLOCK = 128
N_SUBLAYERS = 3
RMS_EPS = 1e-6
NEG_INF = -1e30
FORCE_BONUS = 1e4
NSA_KV_WIDTH = NSA_KV_GROUPS * NSA_HEAD_DIM

MXU_DTYPE = jnp.bfloat16
VMEM_LIMIT_BYTES = 56 * 1024 * 1024

COL_AM = 0
COL_QN = 4096
COL_QD = 5120
COL_KC, COL_VC, COL_KS, COL_VS, COL_KW, COL_VW = 6144, 6400, 6656, 6912, 7168, 7424
COL_QI = 7680
COL_CKV = 8192
COL_SMALL = 8448
N_IN_PAD = 8704


def _dot(a, b):
    return jnp.dot(a, b, preferred_element_type=jnp.float32)


def _rms_scale(x):
    return x * lax.rsqrt(jnp.mean(x * x, axis=-1, keepdims=True) + RMS_EPS)


def _modulated_norm(x, g, shift, scale):
    return (_rms_scale(x) * g) * (1.0 + scale) + shift


def _adaln_kernel(c_ref, w_ref, b_ref, o_ref):
    c = c_ref[...]
    c_act = c * jax.nn.sigmoid(c)
    o_ref[0] = jnp.dot(c_act, w_ref[0], preferred_element_type=jnp.float32,
                       precision=lax.Precision.HIGHEST) + b_ref[0]


def _adaln(c, w_ada, b_ada):
    depth, d, n = w_ada.shape
    b = c.shape[0]
    tn = 1024
    return pl.pallas_call(
        _adaln_kernel,
        grid=(depth, n // tn),
        in_specs=[
            pl.BlockSpec((b, d), lambda l, j: (0, 0)),
            pl.BlockSpec((1, d, tn), lambda l, j: (l, 0, j)),
            pl.BlockSpec((1, 1, tn), lambda l, j: (l, 0, j)),
        ],
        out_specs=pl.BlockSpec((1, b, tn), lambda l, j: (l, 0, j)),
        out_shape=jax.ShapeDtypeStruct((depth, b, n), jnp.float32),
        compiler_params=pltpu.CompilerParams(
            dimension_semantics=("arbitrary", "arbitrary"), vmem_limit_bytes=VMEM_LIMIT_BYTES),
        name="adaln",
    )(c, w_ada, b_ada.reshape(depth, 1, n))


def _ffn_kernel(x_ref, shift_ref, scale_ref, gate_ref, g_ref, wg_ref, wu_ref, wo_ref, gfin_ref, o_ref,
                h_scr, acc_scr, *, final_norm):
    j = pl.program_id(1)

    @pl.when(j == 0)
    def _():
        h = _modulated_norm(x_ref[...], g_ref[...], shift_ref[0], scale_ref[0])
        h_scr[...] = h.astype(h_scr.dtype)
        acc_scr[...] = jnp.zeros_like(acc_scr)

    h = h_scr[...]
    g = _dot(h, wg_ref[...])
    u = _dot(h, wu_ref[...])
    a = (g * jax.nn.sigmoid(g)) * u
    acc_scr[...] += _dot(a.astype(wo_ref.dtype), wo_ref[...])

    @pl.when(j == pl.num_programs(1) - 1)
    def _():
        y = x_ref[...] + (0.5 * gate_ref[0]) * acc_scr[...]
        if final_norm:
            y = _rms_scale(y) * gfin_ref[...]
        o_ref[...] = y


def _ffn(x2d, shift, scale, gate, g, w_i, w_o, g_final, *, seq, final_norm):
    n, d = x2d.shape
    tm, tf = 512, 512
    nf = D_FF // tf
    per_b = seq // tm
    mod_spec = pl.BlockSpec((1, 1, d), lambda i, j: (i // per_b, 0, 0))
    return pl.pallas_call(
        functools.partial(_ffn_kernel, final_norm=final_norm),
        grid=(n // tm, nf),
        in_specs=[
            pl.BlockSpec((tm, d), lambda i, j: (i, 0)),
            mod_spec, mod_spec, mod_spec,
            pl.BlockSpec((1, d), lambda i, j: (0, 0)),
            pl.BlockSpec((d, tf), lambda i, j: (0, j)),
            pl.BlockSpec((d, tf), lambda i, j: (0, j + nf)),
            pl.BlockSpec((tf, d), lambda i, j: (j, 0)),
            pl.BlockSpec((1, d), lambda i, j: (0, 0)),
        ],
        out_specs=pl.BlockSpec((tm, d), lambda i, j: (i, 0)),
        out_shape=jax.ShapeDtypeStruct((n, d), jnp.float32),
        scratch_shapes=[pltpu.VMEM((tm, d), MXU_DTYPE), pltpu.VMEM((tm, d), jnp.float32)],
        compiler_params=pltpu.CompilerParams(
            dimension_semantics=("arbitrary", "arbitrary"), vmem_limit_bytes=VMEM_LIMIT_BYTES),
        name="ffn",
    )(x2d, shift, scale, gate, g.reshape(1, d), w_i, w_i, w_o, g_final.reshape(1, d))


def _inproj_kernel(x_ref, shift_ref, scale_ref, g_ref, w_ref, o_ref, h_scr):
    @pl.when(pl.program_id(1) == 0)
    def _():
        h = _modulated_norm(x_ref[...], g_ref[...], shift_ref[0], scale_ref[0])
        h_scr[...] = h.astype(h_scr.dtype)

    o_ref[...] = _dot(h_scr[...], w_ref[...]).astype(o_ref.dtype)


def _inproj(x2d, shift, scale, g, w, *, seq):
    n, d = x2d.shape
    tm, tn = 512, 512
    per_b = seq // tm
    mod_spec = pl.BlockSpec((1, 1, d), lambda i, j: (i // per_b, 0, 0))
    return pl.pallas_call(
        _inproj_kernel,
        grid=(n // tm, N_IN_PAD // tn),
        in_specs=[
            pl.BlockSpec((tm, d), lambda i, j: (i, 0)),
            mod_spec, mod_spec,
            pl.BlockSpec((1, d), lambda i, j: (0, 0)),
            pl.BlockSpec((d, tn), lambda i, j: (0, j)),
        ],
        out_specs=pl.BlockSpec((tm, tn), lambda i, j: (i, j)),
        out_shape=jax.ShapeDtypeStruct((n, N_IN_PAD), MXU_DTYPE),
        scratch_shapes=[pltpu.VMEM((tm, d), MXU_DTYPE)],
        compiler_params=pltpu.CompilerParams(
            dimension_semantics=("arbitrary", "arbitrary"), vmem_limit_bytes=VMEM_LIMIT_BYTES),
        name="inproj",
    )(x2d, shift, scale, g.reshape(1, d), w)


def _merge_kernel(x_ref, gate_ref, on_ref, od_ref, a0_ref, a1_ref, wun_ref, wud_ref, wout_ref, o_ref):
    yn = _dot(on_ref[...], wun_ref[...])
    yd = _dot(od_ref[...], wud_ref[...])
    a0 = jax.nn.sigmoid(a0_ref[...].astype(jnp.float32))
    a1 = jax.nn.sigmoid(a1_ref[...].astype(jnp.float32))
    y = a0 * yn + a1 * yd
    o_ref[...] = x_ref[...] + gate_ref[0] * _dot(y.astype(wout_ref.dtype), wout_ref[...])


def _merge(x2d, gate, o_n, o_d, proj, w_up_nsa, w_up_dsa, w_out, *, seq):
    n, d = x2d.shape
    tm = 512
    per_b = seq // tm
    hw = o_n.shape[1]
    const = dict(pipeline_mode=pl.Buffered(1))
    return pl.pallas_call(
        _merge_kernel,
        grid=(n // tm,),
        in_specs=[
            pl.BlockSpec((tm, d), lambda i: (i, 0)),
            pl.BlockSpec((1, 1, d), lambda i: (i // per_b, 0, 0)),
            pl.BlockSpec((tm, hw), lambda i: (i, 0)),
            pl.BlockSpec((tm, hw), lambda i: (i, 0)),
            pl.BlockSpec((tm, d), lambda i: (i, COL_AM // d)),
            pl.BlockSpec((tm, d), lambda i: (i, COL_AM // d + 1)),
            pl.BlockSpec((hw, d), lambda i: (0, 0), **const),
            pl.BlockSpec((hw, d), lambda i: (0, 0), **const),
            pl.BlockSpec((d, d), lambda i: (0, 0), **const),
        ],
        out_specs=pl.BlockSpec((tm, d), lambda i: (i, 0)),
        out_shape=jax.ShapeDtypeStruct((n, d), jnp.float32),
        compiler_params=pltpu.CompilerParams(
            dimension_semantics=("arbitrary",), vmem_limit_bytes=VMEM_LIMIT_BYTES),
        name="merge",
    )(x2d, gate, o_n, o_d, proj, proj, w_up_nsa, w_up_dsa, w_out)


def _dot_nt(a, b):
    return lax.dot_general(a, b, (((1,), (1,)), ((), ())), preferred_element_type=jnp.float32)


def _iota(shape, dim):
    return lax.broadcasted_iota(jnp.int32, shape, dim)


def _softmax_rows(s, mask):
    s = jnp.where(mask, s, NEG_INF)
    m = jnp.max(s, axis=1, keepdims=True)
    p = jnp.where(mask, jnp.exp(s - m), 0.0)
    return p / jnp.maximum(jnp.sum(p, axis=1, keepdims=True), 1e-30)


def _online_update(s_parts, mask, v_tile, rows, m_scr, l_scr, acc_scr):
    sm = jnp.concatenate([jnp.where(mask, s, NEG_INF) for s in s_parts], axis=0)
    m_old = m_scr[rows, :]
    m_new = jnp.maximum(m_old, jnp.max(sm, axis=1, keepdims=True))
    alpha = jnp.exp(m_old - m_new)
    p = jnp.exp(sm - m_new)
    l_scr[rows, :] = alpha * l_scr[rows, :] + jnp.sum(p, axis=1, keepdims=True)
    acc_scr[rows, :] = alpha * acc_scr[rows, :] + _dot(p.astype(v_tile.dtype), v_tile)
    m_scr[rows, :] = m_new


def _compress_kernel(x_ref, pe_ref, w1_ref, w2_ref, o_ref, *, transposed):
    x = x_ref[0, 0]
    half = x.shape[1]
    first = _dot(x, w1_ref[0:half, :])
    second = _dot(x, w1_ref[half:2 * half, :])
    pe_term = _dot(pe_ref[...], w1_ref[...])[0:1]
    hid = first + pltpu.roll(second, x.shape[0] - 1, 0) + pe_term
    hid = (hid * jax.nn.sigmoid(hid)).astype(w2_ref.dtype)
    if transposed:
        o_ref[0, 0] = _dot_nt(w2_ref[...], hid).astype(o_ref.dtype)
    else:
        o_ref[0, 0] = _dot(hid, w2_ref[...]).astype(o_ref.dtype)


def _compress(x_chunks, pe, w1, w2, *, transposed):
    b, g, nc, width = x_chunks.shape
    dk = NSA_HEAD_DIM
    pe8 = jnp.zeros((8, 2 * width), MXU_DTYPE).at[0].set(pe.reshape(-1).astype(MXU_DTYPE))
    w2m = (w2.T if transposed else w2).astype(MXU_DTYPE)
    oshape = (b, g, dk, nc) if transposed else (b, g, nc, dk)
    return pl.pallas_call(
        functools.partial(_compress_kernel, transposed=transposed),
        grid=(b, g),
        in_specs=[
            pl.BlockSpec((1, 1, nc, width), lambda i, j: (i, j, 0, 0)),
            pl.BlockSpec((8, 2 * width), lambda i, j: (0, 0)),
            pl.BlockSpec((2 * width, NSA_CMP_HIDDEN), lambda i, j: (0, 0)),
            pl.BlockSpec(w2m.shape, lambda i, j: (0, 0)),
        ],
        out_specs=pl.BlockSpec((1, 1) + oshape[2:], lambda i, j: (i, j, 0, 0)),
        out_shape=jax.ShapeDtypeStruct(oshape, MXU_DTYPE),
        compiler_params=pltpu.CompilerParams(dimension_semantics=("arbitrary", "arbitrary")),
        name="nsa_compress",
    )(x_chunks, pe8, w1.astype(MXU_DTYPE), w2m)


def _ckvnorm_kernel(x_ref, g_ref, o_ref):
    x = x_ref[...].astype(jnp.float32)
    o_ref[...] = (_rms_scale(x) * g_ref[...]).astype(o_ref.dtype)


def _ckvnorm(proj, g_kv):
    n = proj.shape[0]
    tm = 2048
    return pl.pallas_call(
        _ckvnorm_kernel,
        grid=(n // tm,),
        in_specs=[pl.BlockSpec((tm, DSA_KV_LATENT), lambda i: (i, COL_CKV // DSA_KV_LATENT)),
                  pl.BlockSpec((1, DSA_KV_LATENT), lambda i: (0, 0))],
        out_specs=pl.BlockSpec((tm, DSA_KV_LATENT), lambda i: (i, 0)),
        out_shape=jax.ShapeDtypeStruct((n, DSA_KV_LATENT), MXU_DTYPE),
        compiler_params=pltpu.CompilerParams(dimension_semantics=("arbitrary",)),
        name="ckv_norm",
    )(proj, g_kv.reshape(1, -1).astype(jnp.float32))


def _nsa_kernel(far_ref, q_ref, small_ref, kct_ref, vc_ref, kst_ref, vs_ref, kwt_ref, vw_ref, bw_ref, bc_ref,
                ovt_ref, o_ref, m_scr, l_scr, acc_scr, oc_scr, ow_scr):
    i = pl.program_id(1)
    t0 = i * Q_BLOCK
    qb, dk, hg = Q_BLOCK, NSA_HEAD_DIM, NSA_HEADS // NSA_KV_GROUPS
    grows = hg * qb
    nc = kct_ref.shape[2]
    nb = ovt_ref.shape[0]
    wk = qb + NSA_WINDOW
    f32 = jnp.float32

    gates = jax.nn.sigmoid(small_ref[0].astype(f32))
    n_io = _iota((qb, nc), 1)
    tc_io = t0 + _iota((qb, nc), 0)
    mask_c = NSA_CMP_STRIDE * n_io + (NSA_CMP_LEN - 1) <= tc_io
    near_lo = (qb // NSA_CMP_STRIDE) * i - 16
    near_c = n_io >= near_lo
    shift_c = jnp.where(near_lo >= 0, near_lo, near_lo + nc)
    j_io = _iota((nb, qb), 0)
    tj_io = t0 + _iota((nb, qb), 1)
    cur = jnp.right_shift(tj_io, 6)
    forced = (j_io == 0) | (j_io == cur) | (j_io == cur - 1)
    visible = j_io * NSA_SEL_LEN <= tj_io
    eye = jnp.where(_iota((qb, qb), 0) == _iota((qb, qb), 1), 1.0, 0.0).astype(MXU_DTYPE)
    kw_io = _iota((qb, wk), 1)
    off_w = _iota((qb, wk), 0) - kw_io + NSA_WINDOW
    mask_w = (off_w >= 0) & (off_w < NSA_WINDOW) & (t0 - NSA_WINDOW + kw_io >= 0)
    kn_io = _iota((qb, 2 * qb), 1)
    pos_n = t0 - qb + kn_io
    causal_n = (pos_n <= t0 + _iota((qb, 2 * qb), 0)) & (pos_n >= 0)
    n_far = (i + 2) // 4
    far_end = t0 - qb

    for g in range(NSA_KV_GROUPS):
        rows = pl.ds(g * grows, grows)
        gsl = slice(g * dk, (g + 1) * dk)
        qg = q_ref[0, 0, g * grows:(g + 1) * grows, :] * (dk ** -0.5)

        s = _dot(qg, kct_ref[0, gsl, :])
        probs = []
        for r in range(hg):
            h = g * hg + r
            bias = jnp.where(near_c, pltpu.roll(bc_ref[h], shift_c, 1), far_ref[h])
            probs.append(_softmax_rows(s[r * qb:(r + 1) * qb] + bias, mask_c))
        pstack = jnp.concatenate(probs, axis=0)
        oc_scr[rows, :] = _dot(pstack.astype(vc_ref.dtype), vc_ref[0])
        pg = probs[0] + probs[1] + probs[2] + probs[3]
        hi = pg.astype(MXU_DTYPE)
        r1 = pg - hi.astype(f32)
        mid = r1.astype(MXU_DTYPE)
        lo = (r1 - mid.astype(f32)).astype(MXU_DTYPE)
        ovt = ovt_ref[...]
        imp = _dot_nt(ovt, hi) + _dot_nt(ovt, mid) + _dot_nt(ovt, lo)
        imp = jnp.where(visible, imp + jnp.where(forced, FORCE_BONUS, 0.0), NEG_INF)
        rank = jnp.zeros((nb, qb), f32)
        for jp in range(nb):
            row = imp[jp:jp + 1, :]
            beats = jnp.where(row > imp, 1.0, jnp.where(row == imp, jnp.where(jp < j_io, 1.0, 0.0), 0.0))
            rank = rank + beats
        sel_t = jnp.where(rank < NSA_SEL_BLOCKS, 1.0, 0.0).astype(MXU_DTYPE)
        sel = _dot_nt(eye, sel_t).astype(MXU_DTYPE)

        kwt = jnp.concatenate([kwt_ref[0, i + u, gsl, :] for u in range(wk // qb)], axis=1)
        s = _dot(qg, kwt)
        probs = [_softmax_rows(s[r * qb:(r + 1) * qb] + bw_ref[g * hg + r], mask_w) for r in range(hg)]
        pstack = jnp.concatenate(probs, axis=0)
        ow_scr[rows, :] = _dot(pstack.astype(vw_ref.dtype), vw_ref[0, pl.ds(pl.multiple_of(t0, qb), wk), :])

        m_scr[rows, :] = jnp.full((grows, 1), NEG_INF, f32)
        l_scr[rows, :] = jnp.zeros((grows, 1), f32)
        acc_scr[rows, :] = jnp.zeros((grows, acc_scr.shape[1]), f32)

        def far_body(jt, carry, g=g, qg=qg, sel=sel, rows=rows, gsl=gsl):
            kt = jnp.concatenate([kst_ref[0, 1 + 4 * jt + u, gsl, :] for u in range(4)], axis=1)
            vt = vs_ref[0, pl.ds(pl.multiple_of(qb + 512 * jt, qb), 512), :]
            expand = jnp.where(_iota((nb, 512), 0) == 8 * jt + jnp.right_shift(_iota((nb, 512), 1), 6),
                               1.0, 0.0).astype(MXU_DTYPE)
            chosen = _dot(sel, expand)
            pos = 512 * jt + _iota((qb, 512), 1)
            mask = (chosen > 0.5) & (pos < far_end)
            sc = _dot(qg, kt)
            parts = [sc[r * qb:(r + 1) * qb] + far_ref[g * hg + r] for r in range(hg)]
            _online_update(parts, mask, vt, rows, m_scr, l_scr, acc_scr)
            return carry

        lax.fori_loop(0, n_far, far_body, 0)

        kt = jnp.concatenate([kst_ref[0, i + u, gsl, :] for u in range(2)], axis=1)
        vt = vs_ref[0, pl.ds(pl.multiple_of(t0, qb), 2 * qb), :]
        expand = jnp.where(_iota((nb, 2 * qb), 0) == 2 * (i - 1) + jnp.right_shift(_iota((nb, 2 * qb), 1), 6),
                           1.0, 0.0).astype(MXU_DTYPE)
        mask = (_dot(sel, expand) > 0.5) & causal_n
        sc = _dot(qg, kt)
        parts = [sc[r * qb:(r + 1) * qb] + bw_ref[g * hg + r][:, NSA_WINDOW - qb:] for r in range(hg)]
        _online_update(parts, mask, vt, rows, m_scr, l_scr, acc_scr)

        slab = slice((g // 2) * 128, (g // 2) * 128 + 128)
        placed = []
        for r in range(hg):
            h = g * hg + r
            hr = slice(g * grows + r * qb, g * grows + (r + 1) * qb)
            o_s = acc_scr[hr, slab] / l_scr[hr, :]
            o_h = (gates[:, 3 * h:3 * h + 1] * oc_scr[hr, slab] + gates[:, 3 * h + 1:3 * h + 2] * o_s
                   + gates[:, 3 * h + 2:3 * h + 3] * ow_scr[hr, slab])
            if g % 2 != r % 2:
                o_h = pltpu.roll(o_h, dk, 1)
            placed.append(o_h)
        low_half = _iota((qb, 128), 1) < dk
        for pr in range(hg // 2):
            pair = jnp.where(low_half, placed[2 * pr], placed[2 * pr + 1])
            col = (g * hg // 2 + pr) * 128
            o_ref[0, :, col:col + 128] = pair.astype(o_ref.dtype)


def _nsa(proj3, q_r, kct, vc, kst, vs, kwt, vw, bias_w, bias_c, ovt, far):
    b, s, _ = proj3.shape
    nq = s // Q_BLOCK
    rows = NSA_HEADS * Q_BLOCK
    vdim = NSA_KV_WIDTH
    const = dict(pipeline_mode=pl.Buffered(1))

    def whole(a):
        nd = a.ndim
        return pl.BlockSpec((1,) + a.shape[1:], lambda bi, i: (bi,) + (0,) * (nd - 1))

    return pl.pallas_call(
        _nsa_kernel,
        grid=(b, nq),
        in_specs=[
            pl.BlockSpec(memory_space=pltpu.SMEM),
            pl.BlockSpec((1, 1, rows, NSA_HEAD_DIM), lambda bi, i: (bi, i, 0, 0)),
            pl.BlockSpec((1, Q_BLOCK, 128), lambda bi, i: (bi, i, COL_SMALL // 128)),
            whole(kct), whole(vc), whole(kst), whole(vs), whole(kwt), whole(vw),
            pl.BlockSpec(bias_w.shape, lambda bi, i: (0, 0, 0), **const),
            pl.BlockSpec(bias_c.shape, lambda bi, i: (0, 0, 0), **const),
            pl.BlockSpec(ovt.shape, lambda bi, i: (0, 0), **const),
        ],
        out_specs=pl.BlockSpec((1, Q_BLOCK, NSA_HEADS * NSA_HEAD_DIM), lambda bi, i: (bi, i, 0)),
        out_shape=jax.ShapeDtypeStruct((b, s, NSA_HEADS * NSA_HEAD_DIM), MXU_DTYPE),
        scratch_shapes=[pltpu.VMEM((rows, 1), jnp.float32), pltpu.VMEM((rows, 1), jnp.float32),
                        pltpu.VMEM((rows, vdim), jnp.float32), pltpu.VMEM((rows, vdim), jnp.float32),
                        pltpu.VMEM((rows, vdim), jnp.float32)],
        compiler_params=pltpu.CompilerParams(
            dimension_semantics=("arbitrary", "arbitrary"), vmem_limit_bytes=VMEM_LIMIT_BYTES),
        name="nsa_attention",
    )(far, q_r, proj3, kct, vc, kst, vs, kwt, vw, bias_w, bias_c, ovt)


INT_MIN = -2 ** 31


def _sortable_key(x):
    bits = pltpu.bitcast(x, jnp.int32)
    return bits ^ (jnp.right_shift(bits, 31) & 0x7FFFFFFF)


def _dsa_kernel(far_ref, qd_ref, small_ref, qi_ref, kit_ref, ct_ref, c_ref, wuk_ref, wuv_ref, bd_ref, o_ref,
                qlat_scr, key_scr, m_scr, l_scr, acc_scr, tie_scr):
    i = pl.program_id(1)
    t0 = i * Q_BLOCK
    qb = Q_BLOCK
    f32 = jnp.float32
    k_top = float(DSA_TOPK_MAX)
    chunk = 4 * qb
    n_chunks = DSA_HEADS // 4
    t_col = t0 + _iota((qb, 1), 0)

    for pr in range(DSA_HEADS // 2):
        r = _dot(qd_ref[0, :, pr * 128:(pr + 1) * 128], wuk_ref[pr]) * (DSA_HEAD_DIM ** -0.5)
        qlat_scr[2 * pr * qb:(2 * pr + 1) * qb, :] = r[:, :DSA_KV_LATENT].astype(qlat_scr.dtype)
        qlat_scr[(2 * pr + 1) * qb:(2 * pr + 2) * qb, :] = r[:, DSA_KV_LATENT:].astype(qlat_scr.dtype)

    n_tiles = i // 4 + 1
    w_idx = small_ref[0][:, 80:96].astype(f32) * INDEX_SCALE
    key_scr[0] = jnp.full((qb, qb), INT_MIN, jnp.int32)

    def index_body(jt, carry):
        kt = kit_ref[0, jt]
        sc = jnp.zeros((qb, 512), f32)
        for hc in range(n_chunks):
            lg = _dot(qi_ref[0, 0, hc * chunk:(hc + 1) * chunk, :], kt)
            for r in range(4):
                h = hc * 4 + r
                sc = sc + jnp.maximum(lg[r * qb:(r + 1) * qb], 0.0) * w_idx[:, h:h + 1]
        pos = 512 * jt + _iota((qb, 512), 1)
        sc = jnp.where(pos <= t0 + _iota((qb, 512), 0), sc, NEG_INF)
        keys = _sortable_key(sc)
        for u in range(4):
            key_scr[1 + 4 * jt + u] = keys[:, u * qb:(u + 1) * qb]
        return carry

    lax.fori_loop(0, n_tiles, index_body, 0)

    def count(pred):
        def body(jt, c):
            for u in range(4):
                bi = 4 * jt + u
                c = c + jnp.where(pred(key_scr[1 + bi], bi), 1.0, 0.0)
            return c
        c = lax.fori_loop(0, n_tiles, body, jnp.zeros((qb, qb), f32))
        return jnp.sum(c, axis=1, keepdims=True)

    thr = jnp.where(count(lambda kk, bi: kk >= 0) >= k_top, 0, INT_MIN).astype(jnp.int32)
    thr = thr + jnp.zeros((qb, 1), jnp.int32)

    def bit_body(bit, thr):
        cand = thr | jnp.left_shift(jnp.int32(1), 30 - bit)
        return jnp.where(count(lambda kk, bi: kk >= cand) >= k_top, cand, thr)

    thr = lax.fori_loop(0, 31, bit_body, thr)
    need = k_top - count(lambda kk, bi: kk > thr)
    n_ge = count(lambda kk, bi: kk >= thr)
    tie_scr[...] = jnp.full((qb, 1), 2 ** 30, jnp.int32)
    has_excess = jnp.max(jnp.where(n_ge > k_top, 1.0, 0.0)) > 0.5

    @pl.when(has_excess)
    def _():
        def pos_body(bit, last):
            cand = last | jnp.left_shift(jnp.int32(1), 12 - bit)
            n_before = count(lambda kk, bi: (kk == thr) & (bi * qb + _iota((qb, qb), 1) < cand))
            return jnp.where(n_before < need, cand, last)
        tie_scr[...] = lax.fori_loop(0, 13, pos_body, jnp.zeros((qb, 1), jnp.int32))

    last_tie = tie_scr[...]

    def selected(kk, pos):
        return (kk > thr) | ((kk == thr) & (pos <= last_tie))

    m_scr[...] = jnp.full(m_scr.shape, NEG_INF, f32)
    l_scr[...] = jnp.zeros(l_scr.shape, f32)
    acc_scr[...] = jnp.zeros(acc_scr.shape, f32)
    n_far = (i + 2) // 4
    far_end = t0 - qb

    def far_body(jt, carry):
        ct = jnp.concatenate([ct_ref[0, 1 + 4 * jt + u] for u in range(4)], axis=1)
        cv = c_ref[0, pl.ds(pl.multiple_of(qb + 512 * jt, qb), 512), :]
        kk = jnp.concatenate([key_scr[1 + 4 * jt + u] for u in range(4)], axis=1)
        pos = 512 * jt + _iota((qb, 512), 1)
        mask = selected(kk, pos) & (pos < far_end)
        for hc in range(n_chunks):
            rows = pl.ds(hc * chunk, chunk)
            sc = _dot(qlat_scr[hc * chunk:(hc + 1) * chunk, :], ct)
            parts = [sc[r * qb:(r + 1) * qb] + far_ref[NSA_HEADS + hc * 4 + r] for r in range(4)]
            _online_update(parts, mask, cv, rows, m_scr, l_scr, acc_scr)
        return carry

    lax.fori_loop(0, n_far, far_body, 0)

    ct = jnp.concatenate([ct_ref[0, i], ct_ref[0, i + 1]], axis=1)
    cv = c_ref[0, pl.ds(pl.multiple_of(t0, qb), 2 * qb), :]
    kk = jnp.concatenate([key_scr[i], key_scr[i + 1]], axis=1)
    pos = t0 - qb + _iota((qb, 2 * qb), 1)
    mask = selected(kk, pos) & (pos <= t_col) & (pos >= 0)
    for hc in range(n_chunks):
        rows = pl.ds(hc * chunk, chunk)
        sc = _dot(qlat_scr[hc * chunk:(hc + 1) * chunk, :], ct)
        parts = [sc[r * qb:(r + 1) * qb] + bd_ref[hc * 4 + r] for r in range(4)]
        _online_update(parts, mask, cv, rows, m_scr, l_scr, acc_scr)

    for pr in range(DSA_HEADS // 2):
        ra = slice(2 * pr * qb, (2 * pr + 1) * qb)
        rb = slice((2 * pr + 1) * qb, (2 * pr + 2) * qb)
        o_lat = jnp.concatenate([acc_scr[ra, :] / l_scr[ra, :], acc_scr[rb, :] / l_scr[rb, :]], axis=1)
        o_ref[0, :, pr * 128:(pr + 1) * 128] = _dot(o_lat.astype(wuv_ref.dtype), wuv_ref[pr]).astype(o_ref.dtype)


def _dsa(proj3, qi_r, kit, ct, c_pad, wuk_bd, wuv_bd, bias_d, far):
    b, s, _ = proj3.shape
    nq = s // Q_BLOCK
    rows = DSA_HEADS * Q_BLOCK
    const = dict(pipeline_mode=pl.Buffered(1))

    def whole(a):
        nd = a.ndim
        return pl.BlockSpec((1,) + a.shape[1:], lambda bi, i: (bi,) + (0,) * (nd - 1))

    return pl.pallas_call(
        _dsa_kernel,
        grid=(b, nq),
        in_specs=[
            pl.BlockSpec(memory_space=pltpu.SMEM),
            pl.BlockSpec((1, Q_BLOCK, DSA_HEADS * DSA_HEAD_DIM), lambda bi, i: (bi, i, COL_QD // 1024)),
            pl.BlockSpec((1, Q_BLOCK, 128), lambda bi, i: (bi, i, COL_SMALL // 128)),
            pl.BlockSpec((1, 1, rows, IDX_DIM), lambda bi, i: (bi, i, 0, 0)),
            whole(kit), whole(ct), whole(c_pad),
            pl.BlockSpec(wuk_bd.shape, lambda bi, i: (0, 0, 0), **const),
            pl.BlockSpec(wuv_bd.shape, lambda bi, i: (0, 0, 0), **const),
            pl.BlockSpec(bias_d.shape, lambda bi, i: (0, 0, 0), **const),
        ],
        out_specs=pl.BlockSpec((1, Q_BLOCK, DSA_HEADS * DSA_HEAD_DIM), lambda bi, i: (bi, i, 0)),
        out_shape=jax.ShapeDtypeStruct((b, s, DSA_HEADS * DSA_HEAD_DIM), MXU_DTYPE),
        scratch_shapes=[pltpu.VMEM((rows, DSA_KV_LATENT), MXU_DTYPE),
                        pltpu.VMEM((1 + 4 * (-(-nq // 4)), Q_BLOCK, Q_BLOCK), jnp.int32),
                        pltpu.VMEM((rows, 1), jnp.float32), pltpu.VMEM((rows, 1), jnp.float32),
                        pltpu.VMEM((rows, DSA_KV_LATENT), jnp.float32),
                        pltpu.VMEM((Q_BLOCK, 1), jnp.int32)],
        compiler_params=pltpu.CompilerParams(
            dimension_semantics=("arbitrary", "arbitrary"), vmem_limit_bytes=VMEM_LIMIT_BYTES),
        name="dsa_attention",
    )(far, proj3, proj3, qi_r, kit, ct, c_pad, wuk_bd, wuv_bd, bias_d)


def _t5_bucket(dist):
    n = jnp.maximum(dist, 0)
    exact = REL_BUCKETS // 2
    nf = jnp.maximum(n, 1).astype(jnp.float32)
    large = exact + (jnp.log(nf / exact) / math.log(REL_MAX_DIST / exact)
                     * (REL_BUCKETS - exact)).astype(jnp.int32)
    large = jnp.minimum(large, REL_BUCKETS - 1)
    return jnp.where(n < exact, n, large)


def _bias_tables(rel_bias, nc):
    tl = jnp.arange(Q_BLOCK)[:, None]
    per_head = lambda dist, heads: jnp.transpose(rel_bias[_t5_bucket(dist)][..., heads], (2, 0, 1))
    nsa, dsa = slice(0, NSA_HEADS), slice(NSA_HEADS, NSA_HEADS + DSA_HEADS)
    bias_w = per_head(tl - jnp.arange(Q_BLOCK + NSA_WINDOW)[None, :] + NSA_WINDOW, nsa)
    bias_d = per_head(tl - jnp.arange(2 * Q_BLOCK)[None, :] + Q_BLOCK, dsa)
    bias_c = per_head(tl - NSA_CMP_STRIDE * jnp.arange(nc)[None, :] + 225, nsa)
    return bias_w, bias_c, bias_d, rel_bias[REL_BUCKETS - 1]


def _overlap_t(s):
    n_blk, nc = s // NSA_SEL_LEN, s // NSA_CMP_STRIDE
    cmp_start = np.arange(nc) * NSA_CMP_STRIDE
    sel_start = np.arange(n_blk) * NSA_SEL_LEN
    ov = (cmp_start[None, :] < sel_start[:, None] + NSA_SEL_LEN) & (cmp_start[None, :] + NSA_CMP_LEN > sel_start[:, None])
    return jnp.asarray(ov.astype(np.float32)).astype(MXU_DTYPE)


def _block_diag_pairs(w_uk, w_uv):
    hp = DSA_HEADS // 2
    dh, lat = DSA_HEAD_DIM, DSA_KV_LATENT
    uk = jnp.transpose(w_uk, (1, 2, 0)).reshape(hp, 2, dh, lat)
    uv = jnp.transpose(w_uv, (1, 0, 2)).reshape(hp, 2, lat, dh)
    z_k, z_v = jnp.zeros((hp, dh, lat), w_uk.dtype), jnp.zeros((hp, lat, dh), w_uv.dtype)
    wuk_bd = jnp.concatenate([jnp.concatenate([uk[:, 0], z_k], axis=2),
                              jnp.concatenate([z_k, uk[:, 1]], axis=2)], axis=1)
    wuv_bd = jnp.concatenate([jnp.concatenate([uv[:, 0], z_v], axis=2),
                              jnp.concatenate([z_v, uv[:, 1]], axis=2)], axis=1)
    return wuk_bd.astype(MXU_DTYPE), wuv_bd.astype(MXU_DTYPE)


def _mixers(proj, b, s, pe_k, pe_v, wk1, wk2, wv1, wv2, g_kv, w_uk, w_uv, tables, ovt):
    bias_w, bias_c, bias_d, far = tables
    p3 = proj.reshape(b, s, N_IN_PAD)
    nq, nblk = s // Q_BLOCK, s // Q_BLOCK
    g, dk = NSA_KV_GROUPS, NSA_HEAD_DIM

    def piece(off, width):
        return p3[:, :, off:off + width]

    def heads_major(a, heads, dim):
        return a.reshape(b, nq, Q_BLOCK, heads, dim).transpose(0, 1, 3, 2, 4).reshape(b, nq, heads * Q_BLOCK, dim)

    def chunked(a):
        return a.reshape(b, s // 16, 16, g, dk).transpose(0, 3, 1, 2, 4).reshape(b, g, s // 16, 16 * dk)

    def keys_t(a, pad_blocks):
        t = a.reshape(b, nblk, Q_BLOCK, a.shape[-1]).transpose(0, 1, 3, 2)
        return jnp.pad(t, ((0, 0), (pad_blocks, 0), (0, 0), (0, 0)))

    def rows_padded(a, pad):
        return jnp.pad(a, ((0, 0), (pad, 0), (0, 0)))

    kct = _compress(chunked(piece(COL_KC, 256)), pe_k, wk1, wk2, transposed=True).reshape(b, g * dk, s // 16)
    vcm = _compress(chunked(piece(COL_VC, 256)), pe_v, wv1, wv2, transposed=False)
    vcm = vcm.transpose(0, 2, 1, 3).reshape(b, s // 16, g * dk)
    o_n = _nsa(p3, heads_major(piece(COL_QN, 1024), NSA_HEADS, dk), kct, vcm,
               keys_t(piece(COL_KS, 256), 1), rows_padded(piece(COL_VS, 256), Q_BLOCK),
               keys_t(piece(COL_KW, 256), NSA_WINDOW // Q_BLOCK), rows_padded(piece(COL_VW, 256), NSA_WINDOW),
               bias_w, bias_c, ovt, far)

    ckv = _ckvnorm(proj, g_kv).reshape(b, s, DSA_KV_LATENT)
    ki = piece(COL_SMALL + 48, IDX_DIM)
    kit = ki.reshape(b, s // 512, 512, IDX_DIM).transpose(0, 1, 3, 2)
    wuk_bd, wuv_bd = _block_diag_pairs(w_uk, w_uv)
    o_d = _dsa(p3, heads_major(piece(COL_QI, 512), IDX_HEADS, IDX_DIM), kit,
               keys_t(ckv, 1), rows_padded(ckv, Q_BLOCK), wuk_bd, wuv_bd, bias_d, far)
    return o_n.reshape(b * s, -1), o_d.reshape(b * s, -1)


def _permute_w_in(w_in_l):
    d = w_in_l.shape[0]
    sizes = (1024, 256, 256, 256, 256, 256, 256, 48, 1024, 256, 512, 32, 16, 4096)
    offs = np.concatenate([[0], np.cumsum(sizes)])
    (q_n, kc, vc, ks, vs, kw, vw, g_n, q_d, ckv, qi, ki, wi, a_m) = [
        w_in_l[:, int(offs[k]):int(offs[k + 1])] for k in range(len(sizes))]
    small = jnp.concatenate([g_n, ki, wi, jnp.zeros((d, 32), w_in_l.dtype)], axis=1)
    out = jnp.concatenate([a_m, q_n, q_d, kc, vc, ks, vs, kw, vw, qi, ckv, small,
                           jnp.zeros((d, N_IN_PAD - 8576), w_in_l.dtype)], axis=1)
    return out.astype(MXU_DTYPE)


def kernel(x, c, w_ada, b_ada, g_norm, w_ffn_in, w_ffn_out, w_in, nsa_pe_k, nsa_pe_v, nsa_cmp_k1, nsa_cmp_k2,
           nsa_cmp_v1, nsa_cmp_v2, dsa_g_kv, dsa_w_uk, dsa_w_uv, w_up_nsa, w_up_dsa, w_out, rel_bias, g_final):
    B, S, D = x.shape
    N = B * S
    mod = _adaln(c, w_ada, b_ada).reshape(DEPTH, B, N_SUBLAYERS, 3, 1, D)
    tables = _bias_tables(rel_bias, S // NSA_CMP_STRIDE)
    ovt = _overlap_t(S)
    x2d = x.reshape(N, D)
    for l in range(DEPTH):
        m0, m1, m2 = mod[l, :, 0], mod[l, :, 1], mod[l, :, 2]
        x2d = _ffn(x2d, m0[:, 0], m0[:, 1], m0[:, 2], g_norm[l, 0],
                   w_ffn_in[l, 0].astype(MXU_DTYPE), w_ffn_out[l, 0].astype(MXU_DTYPE), g_final,
                   seq=S, final_norm=False)
        proj = _inproj(x2d, m1[:, 0], m1[:, 1], g_norm[l, 1], _permute_w_in(w_in[l]), seq=S)
        o_n, o_d = _mixers(proj, B, S, nsa_pe_k[l], nsa_pe_v[l], nsa_cmp_k1[l], nsa_cmp_k2[l], nsa_cmp_v1[l],
                           nsa_cmp_v2[l], dsa_g_kv[l], dsa_w_uk[l], dsa_w_uv[l], tables, ovt)
        x2d = _merge(x2d, m1[:, 2], o_n, o_d, proj, w_up_nsa[l].astype(MXU_DTYPE),
                     w_up_dsa[l].astype(MXU_DTYPE), w_out[l].astype(MXU_DTYPE), seq=S)
        x2d = _ffn(x2d, m2[:, 0], m2[:, 1], m2[:, 2], g_norm[l, 2],
                   w_ffn_in[l, 1].astype(MXU_DTYPE), w_ffn_out[l, 1].astype(MXU_DTYPE), g_final,
                   seq=S, final_norm=(l == DEPTH - 1))
    return x2d.reshape(B, S, D)
```

```python
import functools
import math

import jax
import jax.numpy as jnp
import numpy as np
from jax import lax
from jax.experimental import pallas as pl
from jax.experimental.pallas import tpu as pltpu

D_MODEL = 2048
DEPTH = 2
NSA_HEADS = 16
NSA_KV_GROUPS = 4
NSA_HEAD_DIM = 64
NSA_CMP_LEN = 32
NSA_CMP_STRIDE = 16
NSA_CMP_HIDDEN = 256
NSA_SEL_LEN = 64
NSA_SEL_BLOCKS = 16
NSA_WINDOW = 512
DSA_HEADS = 16
DSA_HEAD_DIM = 64
DSA_KV_LATENT = 256
IDX_HEADS = 16
IDX_DIM = 32
DSA_TOPK_MAX = 256
INDEX_SCALE = (IDX_HEADS * IDX_DIM) ** -0.5
D_FF = 5632
REL_BUCKETS = 32
REL_MAX_DIST = 128
Q_BLOCK = 128
N_SUBLAYERS = 3
RMS_EPS = 1e-6
NEG_INF = -1e30
FORCE_BONUS = 1e4
NSA_KV_WIDTH = NSA_KV_GROUPS * NSA_HEAD_DIM

MXU_DTYPE = jnp.bfloat16
VMEM_LIMIT_BYTES = 56 * 1024 * 1024

COL_AM = 0
COL_QN = 4096
COL_QD = 5120
COL_KC, COL_VC, COL_KS, COL_VS, COL_KW, COL_VW = 6144, 6400, 6656, 6912, 7168, 7424
COL_QI = 7680
COL_CKV = 8192
COL_SMALL = 8448
N_IN_PAD = 8704


def _dot(a, b):
    return jnp.dot(a, b, preferred_element_type=jnp.float32)


def _rms_scale(x):
    return x * lax.rsqrt(jnp.mean(x * x, axis=-1, keepdims=True) + RMS_EPS)


def _modulated_norm(x, g, shift, scale):
    return (_rms_scale(x) * g) * (1.0 + scale) + shift


def _split_hi_lo(x):
    hi = x.astype(MXU_DTYPE)
    return hi, (x - hi.astype(jnp.float32)).astype(MXU_DTYPE)


def _adaln_kernel(c_ref, w_ref, b_ref, o_ref):
    c = c_ref[...]
    c_hi, c_lo = _split_hi_lo(c * jax.nn.sigmoid(c))
    w_hi, w_lo = _split_hi_lo(w_ref[0])
    o_ref[0] = _dot(c_hi, w_hi) + _dot(c_hi, w_lo) + _dot(c_lo, w_hi) + b_ref[0]


def _adaln(c, w_ada, b_ada):
    depth, d, n = w_ada.shape
    rows = 16
    c = jnp.pad(c, ((0, rows - c.shape[0]), (0, 0)))
    b = rows
    tn = 1024
    return pl.pallas_call(
        _adaln_kernel,
        grid=(depth, n // tn),
        in_specs=[
            pl.BlockSpec((b, d), lambda l, j: (0, 0)),
            pl.BlockSpec((1, d, tn), lambda l, j: (l, 0, j)),
            pl.BlockSpec((1, 1, tn), lambda l, j: (l, 0, j)),
        ],
        out_specs=pl.BlockSpec((1, b, tn), lambda l, j: (l, 0, j)),
        out_shape=jax.ShapeDtypeStruct((depth, b, n), jnp.float32),
        compiler_params=pltpu.CompilerParams(
            dimension_semantics=("arbitrary", "arbitrary"), vmem_limit_bytes=VMEM_LIMIT_BYTES),
        name="adaln",
    )(c, w_ada, b_ada.reshape(depth, 1, n))


def _ffn_kernel(x_ref, shift_ref, scale_ref, gate_ref, g_ref, wg_ref, wu_ref, wo_ref, gfin_ref, o_ref,
                h_scr, acc_scr, *, final_norm):
    j = pl.program_id(1)

    @pl.when(j == 0)
    def _():
        h = _modulated_norm(x_ref[...], g_ref[...], shift_ref[0], scale_ref[0])
        h_scr[...] = h.astype(h_scr.dtype)
        acc_scr[...] = jnp.zeros_like(acc_scr)

    h = h_scr[...]
    g = _dot(h, wg_ref[...])
    u = _dot(h, wu_ref[...])
    a = (g * jax.nn.sigmoid(g)) * u
    acc_scr[...] += _dot(a.astype(wo_ref.dtype), wo_ref[...])

    @pl.when(j == pl.num_programs(1) - 1)
    def _():
        y = x_ref[...] + (0.5 * gate_ref[0]) * acc_scr[...]
        if final_norm:
            y = _rms_scale(y) * gfin_ref[...]
        o_ref[...] = y


def _ffn(x2d, shift, scale, gate, g, w_i, w_o, g_final, *, seq, final_norm):
    n, d = x2d.shape
    tm, tf = 512, 512
    nf = D_FF // tf
    per_b = seq // tm
    mod_spec = pl.BlockSpec((1, 1, d), lambda i, j: (i // per_b, 0, 0))
    return pl.pallas_call(
        functools.partial(_ffn_kernel, final_norm=final_norm),
        grid=(n // tm, nf),
        in_specs=[
            pl.BlockSpec((tm, d), lambda i, j: (i, 0)),
            mod_spec, mod_spec, mod_spec,
            pl.BlockSpec((1, d), lambda i, j: (0, 0)),
            pl.BlockSpec((d, tf), lambda i, j: (0, j)),
            pl.BlockSpec((d, tf), lambda i, j: (0, j + nf)),
            pl.BlockSpec((tf, d), lambda i, j: (j, 0)),
            pl.BlockSpec((1, d), lambda i, j: (0, 0)),
        ],
        out_specs=pl.BlockSpec((tm, d), lambda i, j: (i, 0)),
        out_shape=jax.ShapeDtypeStruct((n, d), jnp.float32),
        scratch_shapes=[pltpu.VMEM((tm, d), MXU_DTYPE), pltpu.VMEM((tm, d), jnp.float32)],
        compiler_params=pltpu.CompilerParams(
            dimension_semantics=("arbitrary", "arbitrary"), vmem_limit_bytes=VMEM_LIMIT_BYTES),
        name="ffn",
    )(x2d, shift, scale, gate, g.reshape(1, d), w_i, w_i, w_o, g_final.reshape(1, d))


def _inproj_kernel(x_ref, shift_ref, scale_ref, g_ref, w_ref, o_ref, h_scr):
    @pl.when(pl.program_id(1) == 0)
    def _():
        h = _modulated_norm(x_ref[...], g_ref[...], shift_ref[0], scale_ref[0])
        h_scr[...] = h.astype(h_scr.dtype)

    o_ref[...] = _dot(h_scr[...], w_ref[...]).astype(o_ref.dtype)


def _inproj(x2d, shift, scale, g, w, *, seq):
    n, d = x2d.shape
    tm, tn = 1024, 512
    per_b = seq // tm
    mod_spec = pl.BlockSpec((1, 1, d), lambda i, j: (i // per_b, 0, 0))
    return pl.pallas_call(
        _inproj_kernel,
        grid=(n // tm, N_IN_PAD // tn),
        in_specs=[
            pl.BlockSpec((tm, d), lambda i, j: (i, 0)),
            mod_spec, mod_spec,
            pl.BlockSpec((1, d), lambda i, j: (0, 0)),
            pl.BlockSpec((d, tn), lambda i, j: (0, j)),
        ],
        out_specs=pl.BlockSpec((tm, tn), lambda i, j: (i, j)),
        out_shape=jax.ShapeDtypeStruct((n, N_IN_PAD), MXU_DTYPE),
        scratch_shapes=[pltpu.VMEM((tm, d), MXU_DTYPE)],
        compiler_params=pltpu.CompilerParams(
            dimension_semantics=("arbitrary", "arbitrary"), vmem_limit_bytes=VMEM_LIMIT_BYTES),
        name="inproj",
    )(x2d, shift, scale, g.reshape(1, d), w)


def _merge_kernel(x_ref, gate_ref, on_ref, od_ref, a0_ref, a1_ref, wun_ref, wud_ref, wout_ref, o_ref):
    yn = _dot(on_ref[...], wun_ref[...])
    yd = _dot(od_ref[...], wud_ref[...])
    a0 = jax.nn.sigmoid(a0_ref[...].astype(jnp.float32))
    a1 = jax.nn.sigmoid(a1_ref[...].astype(jnp.float32))
    y = a0 * yn + a1 * yd
    o_ref[...] = x_ref[...] + gate_ref[0] * _dot(y.astype(wout_ref.dtype), wout_ref[...])


def _merge(x2d, gate, o_n, o_d, proj, w_up_nsa, w_up_dsa, w_out, *, seq):
    n, d = x2d.shape
    tm = 512
    per_b = seq // tm
    hw = o_n.shape[1]
    const = dict(pipeline_mode=pl.Buffered(1))
    return pl.pallas_call(
        _merge_kernel,
        grid=(n // tm,),
        in_specs=[
            pl.BlockSpec((tm, d), lambda i: (i, 0)),
            pl.BlockSpec((1, 1, d), lambda i: (i // per_b, 0, 0)),
            pl.BlockSpec((tm, hw), lambda i: (i, 0)),
            pl.BlockSpec((tm, hw), lambda i: (i, 0)),
            pl.BlockSpec((tm, d), lambda i: (i, COL_AM // d)),
            pl.BlockSpec((tm, d), lambda i: (i, COL_AM // d + 1)),
            pl.BlockSpec((hw, d), lambda i: (0, 0), **const),
            pl.BlockSpec((hw, d), lambda i: (0, 0), **const),
            pl.BlockSpec((d, d), lambda i: (0, 0), **const),
        ],
        out_specs=pl.BlockSpec((tm, d), lambda i: (i, 0)),
        out_shape=jax.ShapeDtypeStruct((n, d), jnp.float32),
        compiler_params=pltpu.CompilerParams(
            dimension_semantics=("arbitrary",), vmem_limit_bytes=VMEM_LIMIT_BYTES),
        name="merge",
    )(x2d, gate, o_n, o_d, proj, proj, w_up_nsa, w_up_dsa, w_out)


def _dot_nt(a, b):
    return lax.dot_general(a, b, (((1,), (1,)), ((), ())), preferred_element_type=jnp.float32)


def _iota(shape, dim):
    return lax.broadcasted_iota(jnp.int32, shape, dim)


def _softmax_rows(s, mask):
    s = jnp.where(mask, s, NEG_INF)
    m = jnp.max(s, axis=1, keepdims=True)
    p = jnp.where(mask, jnp.exp(s - m), 0.0)
    return p / jnp.maximum(jnp.sum(p, axis=1, keepdims=True), 1e-30)


def _lane_tiles(x):
    return [x[:, u * 128:(u + 1) * 128] for u in range(x.shape[1] // 128)]


def _max_pass(s_parts, mask, rows0, mx_scr):
    qb = s_parts[0].shape[0]
    for r, s in enumerate(s_parts):
        tiles = _lane_tiles(jnp.where(mask, s, NEG_INF))
        red = functools.reduce(jnp.maximum, tiles)
        sl = slice(rows0 + r * qb, rows0 + (r + 1) * qb)
        mx_scr[sl, :] = jnp.maximum(mx_scr[sl, :], red)


def _exp_pass(s_parts, mask, shifts, v_tile, rows0, ls_scr, acc_scr):
    qb = s_parts[0].shape[0]
    ps = []
    for r, (s, shift) in enumerate(zip(s_parts, shifts)):
        if mask is not None:
            s = jnp.where(mask, s, NEG_INF)
        ptiles = [jnp.exp(t - shift) for t in _lane_tiles(s)]
        sl = slice(rows0 + r * qb, rows0 + (r + 1) * qb)
        ls_scr[sl, :] = ls_scr[sl, :] + functools.reduce(jnp.add, ptiles)
        ps.append(jnp.concatenate(ptiles, axis=1))
    p = jnp.concatenate(ps, axis=0).astype(v_tile.dtype)
    rows = slice(rows0, rows0 + len(s_parts) * qb)
    acc_scr[rows, :] = acc_scr[rows, :] + _dot(p, v_tile)


def _compress_kernel(x_ref, pe_ref, w1_ref, w2_ref, o_ref, *, transposed):
    x = x_ref[0, 0]
    half = x.shape[1]
    first = _dot(x, w1_ref[0:half, :])
    second = _dot(x, w1_ref[half:2 * half, :])
    pe_term = _dot(pe_ref[...], w1_ref[...])[0:1]
    hid = first + pltpu.roll(second, x.shape[0] - 1, 0) + pe_term
    hid = (hid * jax.nn.sigmoid(hid)).astype(w2_ref.dtype)
    if transposed:
        o_ref[0, 0] = _dot_nt(w2_ref[...], hid).astype(o_ref.dtype)
    else:
        o_ref[0, 0] = _dot(hid, w2_ref[...]).astype(o_ref.dtype)


def _compress(x_chunks, pe, w1, w2, *, transposed):
    b, g, nc, width = x_chunks.shape
    dk = NSA_HEAD_DIM
    pe8 = jnp.zeros((8, 2 * width), MXU_DTYPE).at[0].set(pe.reshape(-1).astype(MXU_DTYPE))
    w2m = (w2.T if transposed else w2).astype(MXU_DTYPE)
    oshape = (b, g, dk, nc) if transposed else (b, g, nc, dk)
    return pl.pallas_call(
        functools.partial(_compress_kernel, transposed=transposed),
        grid=(b, g),
        in_specs=[
            pl.BlockSpec((1, 1, nc, width), lambda i, j: (i, j, 0, 0)),
            pl.BlockSpec((8, 2 * width), lambda i, j: (0, 0)),
            pl.BlockSpec((2 * width, NSA_CMP_HIDDEN), lambda i, j: (0, 0)),
            pl.BlockSpec(w2m.shape, lambda i, j: (0, 0)),
        ],
        out_specs=pl.BlockSpec((1, 1) + oshape[2:], lambda i, j: (i, j, 0, 0)),
        out_shape=jax.ShapeDtypeStruct(oshape, MXU_DTYPE),
        compiler_params=pltpu.CompilerParams(dimension_semantics=("arbitrary", "arbitrary")),
        name="nsa_compress",
    )(x_chunks, pe8, w1.astype(MXU_DTYPE), w2m)


def _ckvnorm_kernel(x_ref, g_ref, o_ref):
    x = x_ref[...].astype(jnp.float32)
    o_ref[...] = (_rms_scale(x) * g_ref[...]).astype(o_ref.dtype)


def _ckvnorm(proj, g_kv):
    n = proj.shape[0]
    tm = 2048
    return pl.pallas_call(
        _ckvnorm_kernel,
        grid=(n // tm,),
        in_specs=[pl.BlockSpec((tm, DSA_KV_LATENT), lambda i: (i, COL_CKV // DSA_KV_LATENT)),
                  pl.BlockSpec((1, DSA_KV_LATENT), lambda i: (0, 0))],
        out_specs=pl.BlockSpec((tm, DSA_KV_LATENT), lambda i: (i, 0)),
        out_shape=jax.ShapeDtypeStruct((n, DSA_KV_LATENT), MXU_DTYPE),
        compiler_params=pltpu.CompilerParams(dimension_semantics=("arbitrary",)),
        name="ckv_norm",
    )(proj, g_kv.reshape(1, -1).astype(jnp.float32))


def _nsa_kernel(far_ref, q_ref, small_ref, kct_ref, vc_ref, kst_ref, vs_ref, kwt_ref, vw_ref, bw_ref, bc_ref,
                ovt_ref, o_ref, qs_scr, sel_scr, mxf_scr, mb_scr, ls_scr, acc_scr, sn_scr, oc_scr, ow_scr):
    i = pl.program_id(1)
    t0 = i * Q_BLOCK
    qb, dk, hg = Q_BLOCK, NSA_HEAD_DIM, NSA_HEADS // NSA_KV_GROUPS
    grows = hg * qb
    nc = kct_ref.shape[2]
    nb = ovt_ref.shape[0]
    wk = qb + NSA_WINDOW
    f32 = jnp.float32

    gates = jax.nn.sigmoid(small_ref[0].astype(f32))
    n_io = _iota((qb, nc), 1)
    tc_io = t0 + _iota((qb, nc), 0)
    mask_c = NSA_CMP_STRIDE * n_io + (NSA_CMP_LEN - 1) <= tc_io
    near_lo = (qb // NSA_CMP_STRIDE) * i - 16
    near_c = n_io >= near_lo
    shift_c = jnp.where(near_lo >= 0, near_lo, near_lo + nc)
    j_io = _iota((nb, qb), 0)
    tj_io = t0 + _iota((nb, qb), 1)
    cur = jnp.right_shift(tj_io, 6)
    forced = (j_io == 0) | (j_io == cur) | (j_io == cur - 1)
    visible = j_io * NSA_SEL_LEN <= tj_io
    eye = jnp.where(_iota((qb, qb), 0) == _iota((qb, qb), 1), 1.0, 0.0).astype(MXU_DTYPE)
    kw_io = _iota((qb, wk), 1)
    off_w = _iota((qb, wk), 0) - kw_io + NSA_WINDOW
    mask_w = (off_w >= 0) & (off_w < NSA_WINDOW) & (t0 - NSA_WINDOW + kw_io >= 0)
    kn_io = _iota((qb, 2 * qb), 1)
    pos_n = t0 - qb + kn_io
    causal_n = (pos_n <= t0 + _iota((qb, 2 * qb), 0)) & (pos_n >= 0)
    n_far = (i + 2) // 4
    far_blocks = 2 * (i - 1)
    jsel_io = _iota((qb, nb), 1)
    qs_scr[...] = q_ref[0, 0] * (dk ** -0.5)

    for g in range(NSA_KV_GROUPS):
        rows = pl.ds(g * grows, grows)
        gsl = slice(g * dk, (g + 1) * dk)
        qg = qs_scr[g * grows:(g + 1) * grows, :]

        s = _dot(qg, kct_ref[0, gsl, :])
        probs = []
        for r in range(hg):
            h = g * hg + r
            bias = jnp.where(near_c, pltpu.roll(bc_ref[h], shift_c, 1), far_ref[h])
            probs.append(_softmax_rows(s[r * qb:(r + 1) * qb] + bias, mask_c))
        pstack = jnp.concatenate(probs, axis=0)
        oc_scr[rows, :] = _dot(pstack.astype(vc_ref.dtype), vc_ref[0])
        pg = probs[0] + probs[1] + probs[2] + probs[3]
        hi = pg.astype(MXU_DTYPE)
        r1 = pg - hi.astype(f32)
        mid = r1.astype(MXU_DTYPE)
        lo = (r1 - mid.astype(f32)).astype(MXU_DTYPE)
        ovt = ovt_ref[...]
        imp = _dot_nt(ovt, hi) + _dot_nt(ovt, mid) + _dot_nt(ovt, lo)
        imp = jnp.where(visible, imp + jnp.where(forced, FORCE_BONUS, 0.0), NEG_INF)
        rank = jnp.zeros((nb, qb), f32)
        for jp in range(nb):
            row = imp[jp:jp + 1, :]
            beats = jnp.where(row > imp, 1.0, jnp.where(row == imp, jnp.where(jp < j_io, 1.0, 0.0), 0.0))
            rank = rank + beats
        sel_t = jnp.where(rank < NSA_SEL_BLOCKS, 1.0, 0.0).astype(MXU_DTYPE)
        sel = _dot_nt(eye, sel_t)
        sel_scr[2 * g] = sel
        sel_scr[2 * g + 1] = jnp.where(jsel_io < far_blocks, sel, 0.0)

        kwt = jnp.concatenate([kwt_ref[0, i + u, gsl, :] for u in range(wk // qb)], axis=1)
        s = _dot(qg, kwt)
        probs, sums = [], []
        for r in range(hg):
            sh = jnp.where(mask_w, s[r * qb:(r + 1) * qb] + bw_ref[g * hg + r], NEG_INF)
            p = jnp.exp(sh - jnp.max(sh, axis=1, keepdims=True))
            probs.append(p)
            sums.append(jnp.sum(p, axis=1, keepdims=True))
        pstack = jnp.concatenate(probs, axis=0)
        ow = _dot(pstack.astype(vw_ref.dtype), vw_ref[0, pl.ds(pl.multiple_of(t0, qb), wk), :])
        for r in range(hg):
            ow_scr[g * grows + r * qb:g * grows + (r + 1) * qb, :] = ow[r * qb:(r + 1) * qb] / sums[r]

    def far_tile(jt, g):
        gsl = slice(g * dk, (g + 1) * dk)
        kt = jnp.concatenate([kst_ref[0, 1 + 4 * jt + u, gsl, :] for u in range(4)], axis=1)
        expand = jnp.where(_iota((nb, 512), 0) == 8 * jt + jnp.right_shift(_iota((nb, 512), 1), 6),
                           1.0, 0.0).astype(MXU_DTYPE)
        mask = _dot(sel_scr[2 * g + 1].astype(MXU_DTYPE), expand) > 0.5
        sc = _dot(qs_scr[g * grows:(g + 1) * grows, :], kt)
        return [sc[r * qb:(r + 1) * qb] for r in range(hg)], mask

    mxf_scr[...] = jnp.full(mxf_scr.shape, NEG_INF, f32)
    ls_scr[...] = jnp.zeros(ls_scr.shape, f32)
    acc_scr[...] = jnp.zeros(acc_scr.shape, f32)

    def far_max_body(jt, carry):
        for g in range(NSA_KV_GROUPS):
            parts, mask = far_tile(jt, g)
            _max_pass(parts, mask, g * grows, mxf_scr)
        return carry

    lax.fori_loop(0, n_far, far_max_body, 0)

    expand_n = jnp.where(_iota((nb, 2 * qb), 0) == 2 * (i - 1) + jnp.right_shift(_iota((nb, 2 * qb), 1), 6),
                         1.0, 0.0).astype(MXU_DTYPE)
    for g in range(NSA_KV_GROUPS):
        gsl = slice(g * dk, (g + 1) * dk)
        kt = jnp.concatenate([kst_ref[0, i + u, gsl, :] for u in range(2)], axis=1)
        mask = (_dot(sel_scr[2 * g].astype(MXU_DTYPE), expand_n) > 0.5) & causal_n
        sc = _dot(qs_scr[g * grows:(g + 1) * grows, :], kt)
        for r in range(hg):
            h = g * hg + r
            hr = slice(g * grows + r * qb, g * grows + (r + 1) * qb)
            sn = jnp.where(mask, sc[r * qb:(r + 1) * qb] + bw_ref[h][:, NSA_WINDOW - qb:], NEG_INF)
            sn_scr[hr, :] = sn
            m_far = jnp.max(mxf_scr[hr, :], axis=1, keepdims=True) + far_ref[h]
            m_row = jnp.maximum(m_far, jnp.max(sn, axis=1, keepdims=True))
            mb_scr[hr, :] = jnp.broadcast_to(m_row, (qb, 128))

    def far_exp_body(jt, carry):
        vt = vs_ref[0, pl.ds(pl.multiple_of(qb + 512 * jt, qb), 512), :]
        for g in range(NSA_KV_GROUPS):
            parts, mask = far_tile(jt, g)
            shifts = [mb_scr[g * grows + r * qb:g * grows + (r + 1) * qb, :] - far_ref[g * hg + r]
                      for r in range(hg)]
            _exp_pass(parts, mask, shifts, vt, g * grows, ls_scr, acc_scr)
        return carry

    lax.fori_loop(0, n_far, far_exp_body, 0)

    vt_n = vs_ref[0, pl.ds(pl.multiple_of(t0, qb), 2 * qb), :]
    for g in range(NSA_KV_GROUPS):
        hrs = [slice(g * grows + r * qb, g * grows + (r + 1) * qb) for r in range(hg)]
        _exp_pass([sn_scr[hr, :] for hr in hrs], None, [mb_scr[hr, :] for hr in hrs], vt_n, g * grows,
                  ls_scr, acc_scr)

    for g in range(NSA_KV_GROUPS):
        slab = slice((g // 2) * 128, (g // 2) * 128 + 128)
        placed = []
        for r in range(hg):
            h = g * hg + r
            hr = slice(g * grows + r * qb, g * grows + (r + 1) * qb)
            o_s = acc_scr[hr, slab] / jnp.sum(ls_scr[hr, :], axis=1, keepdims=True)
            o_h = (gates[:, 3 * h:3 * h + 1] * oc_scr[hr, slab] + gates[:, 3 * h + 1:3 * h + 2] * o_s
                   + gates[:, 3 * h + 2:3 * h + 3] * ow_scr[hr, slab])
            if g % 2 != r % 2:
                o_h = pltpu.roll(o_h, dk, 1)
            placed.append(o_h)
        low_half = _iota((qb, 128), 1) < dk
        for pr in range(hg // 2):
            pair = jnp.where(low_half, placed[2 * pr], placed[2 * pr + 1])
            col = (g * hg // 2 + pr) * 128
            o_ref[0, :, col:col + 128] = pair.astype(o_ref.dtype)


def _nsa(proj3, q_r, kct, vc, kst, vs, kwt, vw, bias_w, bias_c, ovt, far):
    b, s, _ = proj3.shape
    nq = s // Q_BLOCK
    rows = NSA_HEADS * Q_BLOCK
    vdim = NSA_KV_WIDTH
    const = dict(pipeline_mode=pl.Buffered(1))

    def whole(a):
        nd = a.ndim
        return pl.BlockSpec((1,) + a.shape[1:], lambda bi, i: (bi,) + (0,) * (nd - 1))

    return pl.pallas_call(
        _nsa_kernel,
        grid=(b, nq),
        in_specs=[
            pl.BlockSpec(memory_space=pltpu.SMEM),
            pl.BlockSpec((1, 1, rows, NSA_HEAD_DIM), lambda bi, i: (bi, i, 0, 0)),
            pl.BlockSpec((1, Q_BLOCK, 128), lambda bi, i: (bi, i, COL_SMALL // 128)),
            whole(kct), whole(vc), whole(kst), whole(vs), whole(kwt), whole(vw),
            pl.BlockSpec(bias_w.shape, lambda bi, i: (0, 0, 0), **const),
            pl.BlockSpec(bias_c.shape, lambda bi, i: (0, 0, 0), **const),
            pl.BlockSpec(ovt.shape, lambda bi, i: (0, 0), **const),
        ],
        out_specs=pl.BlockSpec((1, Q_BLOCK, NSA_HEADS * NSA_HEAD_DIM), lambda bi, i: (bi, i, 0)),
        out_shape=jax.ShapeDtypeStruct((b, s, NSA_HEADS * NSA_HEAD_DIM), MXU_DTYPE),
        scratch_shapes=[pltpu.VMEM((rows, NSA_HEAD_DIM), MXU_DTYPE),
                        pltpu.VMEM((2 * NSA_KV_GROUPS, Q_BLOCK, ovt.shape[0]), jnp.float32),
                        pltpu.VMEM((rows, 128), jnp.float32),
                        pltpu.VMEM((rows, 128), jnp.float32),
                        pltpu.VMEM((rows, 128), jnp.float32),
                        pltpu.VMEM((rows, vdim), jnp.float32),
                        pltpu.VMEM((rows, 2 * Q_BLOCK), jnp.float32),
                        pltpu.VMEM((rows, vdim), jnp.float32),
                        pltpu.VMEM((rows, vdim), jnp.float32)],
        compiler_params=pltpu.CompilerParams(
            dimension_semantics=("arbitrary", "arbitrary"), vmem_limit_bytes=VMEM_LIMIT_BYTES),
        name="nsa_attention",
    )(far, q_r, proj3, kct, vc, kst, vs, kwt, vw, bias_w, bias_c, ovt)


INT_MIN = -2 ** 31


def _sortable_key(x):
    bits = pltpu.bitcast(x, jnp.int32)
    return bits ^ (jnp.right_shift(bits, 31) & 0x7FFFFFFF)


def _dsa_kernel(far_ref, qd_ref, qit_ref, wt_ref, ki_ref, ct_ref, c_ref, wuk_ref, wuv_ref, bd_ref, o_ref,
                qlat_scr, keyt_scr, key_scr, mxf_scr, mb_scr, ls_scr, acc_scr, sn_scr, tie_scr):
    i = pl.program_id(1)
    t0 = i * Q_BLOCK
    qb = Q_BLOCK
    f32 = jnp.float32
    k_top = float(DSA_TOPK_MAX)
    chunk = 4 * qb
    n_chunks = DSA_HEADS // 4

    for pr in range(DSA_HEADS // 2):
        r = _dot(qd_ref[0, :, pr * 128:(pr + 1) * 128], wuk_ref[pr]) * (DSA_HEAD_DIM ** -0.5)
        qlat_scr[2 * pr * qb:(2 * pr + 1) * qb, :] = r[:, :DSA_KV_LATENT].astype(qlat_scr.dtype)
        qlat_scr[(2 * pr + 1) * qb:(2 * pr + 2) * qb, :] = r[:, DSA_KV_LATENT:].astype(qlat_scr.dtype)

    n_tiles = i // 4 + 1
    w_t = wt_ref[0, 0].astype(f32) * INDEX_SCALE
    tq_io = t0 + _iota((512, qb), 1)
    keyt_scr[0:qb, :] = jnp.full((qb, qb), INT_MIN, jnp.int32)

    def tile_rows(jt):
        return pl.ds(pl.multiple_of(qb + 512 * jt, qb), 512)

    def index_body(jt, carry):
        kt = ki_ref[0, pl.ds(pl.multiple_of(512 * jt, 512), 512), :]
        sc = jnp.zeros((512, qb), f32)
        for hc in range(n_chunks):
            lg = _dot(kt, qit_ref[0, 0, :, hc * chunk:(hc + 1) * chunk])
            for r in range(4):
                h = hc * 4 + r
                sc = sc + jnp.maximum(lg[:, r * qb:(r + 1) * qb], 0.0) * w_t[h:h + 1, :]
        sc = jnp.where(512 * jt + _iota((512, qb), 0) <= tq_io, sc, NEG_INF)
        keyt_scr[tile_rows(jt), :] = _sortable_key(sc)
        return carry

    lax.fori_loop(0, n_tiles, index_body, 0)

    def count(pred):
        def body(jt, c):
            x = jnp.where(pred(keyt_scr[tile_rows(jt), :], 512 * jt + _iota((512, qb), 0)), 1.0, 0.0)
            parts = [x[8 * v:8 * v + 8] for v in range(512 // 8)]
            while len(parts) > 1:
                parts = [a + b for a, b in zip(parts[0::2], parts[1::2])]
            return c + parts[0]
        c = lax.fori_loop(0, n_tiles, body, jnp.zeros((8, qb), f32))
        return jnp.sum(c, axis=0, keepdims=True)

    thr = jnp.where(count(lambda kk, pos: kk >= 0) >= k_top, 0, INT_MIN).astype(jnp.int32)
    thr = thr + jnp.zeros((1, qb), jnp.int32)

    def bit_body(bit, thr):
        cand = thr | jnp.left_shift(jnp.int32(1), 30 - bit)
        return jnp.where(count(lambda kk, pos: kk >= cand) >= k_top, cand, thr)

    thr = lax.fori_loop(0, 31, bit_body, thr)
    need = k_top - count(lambda kk, pos: kk > thr)
    n_ge = count(lambda kk, pos: kk >= thr)
    tie_scr[...] = jnp.full((1, qb), 2 ** 30, jnp.int32)
    has_excess = jnp.max(jnp.where(n_ge > k_top, 1.0, 0.0)) > 0.5

    @pl.when(has_excess)
    def _():
        def pos_body(bit, last):
            cand = last | jnp.left_shift(jnp.int32(1), 12 - bit)
            n_before = count(lambda kk, pos: (kk == thr) & (pos < cand))
            return jnp.where(n_before < need, cand, last)
        tie_scr[...] = lax.fori_loop(0, 13, pos_body, jnp.zeros((1, qb), jnp.int32))

    last_tie = tie_scr[...]
    eye = jnp.where(_iota((qb, qb), 0) == _iota((qb, qb), 1), 1.0, 0.0).astype(MXU_DTYPE)

    def selected_rows(kk, pos, extra=None):
        chosen = (kk > thr) | ((kk == thr) & (pos <= last_tie))
        if extra is not None:
            chosen = chosen & extra
        return _dot_nt(eye, jnp.where(chosen, 1.0, 0.0).astype(MXU_DTYPE)) > 0.5

    pos_n = t0 - qb + _iota((2 * qb, qb), 0)
    causal_n = (pos_n <= t0 + _iota((2 * qb, qb), 1)) & (pos_n >= 0)
    mask_n = selected_rows(keyt_scr[pl.ds(pl.multiple_of(t0, qb), 2 * qb), :], pos_n, causal_n)

    def mask_body(jt, carry):
        mt = selected_rows(keyt_scr[tile_rows(jt), :], 512 * jt + _iota((512, qb), 0))
        for u in range(4):
            bi = 4 * jt + u
            key_scr[1 + bi] = jnp.where(mt[:, u * qb:(u + 1) * qb], jnp.where(bi < i - 1, 1, 0), 0)
        return carry

    lax.fori_loop(0, n_tiles, mask_body, 0)

    mxf_scr[...] = jnp.full(mxf_scr.shape, NEG_INF, f32)
    ls_scr[...] = jnp.zeros(ls_scr.shape, f32)
    acc_scr[...] = jnp.zeros(acc_scr.shape, f32)
    n_far = (i + 2) // 4

    def far_tile(jt, hc):
        ct = jnp.concatenate([ct_ref[0, 1 + 4 * jt + u] for u in range(4)], axis=1)
        sc = _dot(qlat_scr[hc * chunk:(hc + 1) * chunk, :], ct)
        return [sc[r * qb:(r + 1) * qb] for r in range(4)]

    def far_mask(jt):
        return jnp.concatenate([key_scr[1 + 4 * jt + u] for u in range(4)], axis=1) > 0

    def far_max_body(jt, carry):
        mask = far_mask(jt)
        for hc in range(n_chunks):
            _max_pass(far_tile(jt, hc), mask, hc * chunk, mxf_scr)
        return carry

    lax.fori_loop(0, n_far, far_max_body, 0)

    ct_n = jnp.concatenate([ct_ref[0, i], ct_ref[0, i + 1]], axis=1)
    for hc in range(n_chunks):
        sc = _dot(qlat_scr[hc * chunk:(hc + 1) * chunk, :], ct_n)
        for r in range(4):
            h = hc * 4 + r
            hr = slice(h * qb, (h + 1) * qb)
            sn = jnp.where(mask_n, sc[r * qb:(r + 1) * qb] + bd_ref[h], NEG_INF)
            sn_scr[hr, :] = sn
            m_far = jnp.max(mxf_scr[hr, :], axis=1, keepdims=True) + far_ref[NSA_HEADS + h]
            m_row = jnp.maximum(m_far, jnp.max(sn, axis=1, keepdims=True))
            mb_scr[hr, :] = jnp.broadcast_to(m_row, (qb, 128))

    def far_exp_body(jt, carry):
        mask = far_mask(jt)
        cv = c_ref[0, pl.ds(pl.multiple_of(qb + 512 * jt, qb), 512), :]
        for hc in range(n_chunks):
            shifts = [mb_scr[(hc * 4 + r) * qb:(hc * 4 + r + 1) * qb, :] - far_ref[NSA_HEADS + hc * 4 + r]
                      for r in range(4)]
            _exp_pass(far_tile(jt, hc), mask, shifts, cv, hc * chunk, ls_scr, acc_scr)
        return carry

    lax.fori_loop(0, n_far, far_exp_body, 0)

    cv_n = c_ref[0, pl.ds(pl.multiple_of(t0, qb), 2 * qb), :]
    for hc in range(n_chunks):
        hrs = [slice((hc * 4 + r) * qb, (hc * 4 + r + 1) * qb) for r in range(4)]
        _exp_pass([sn_scr[hr, :] for hr in hrs], None, [mb_scr[hr, :] for hr in hrs], cv_n, hc * chunk,
                  ls_scr, acc_scr)

    def normalised(hr):
        return acc_scr[hr, :] / jnp.sum(ls_scr[hr, :], axis=1, keepdims=True)

    for pr in range(DSA_HEADS // 2):
        ra = slice(2 * pr * qb, (2 * pr + 1) * qb)
        rb = slice((2 * pr + 1) * qb, (2 * pr + 2) * qb)
        o_lat = jnp.concatenate([normalised(ra), normalised(rb)], axis=1)
        o_ref[0, :, pr * 128:(pr + 1) * 128] = _dot(o_lat.astype(wuv_ref.dtype), wuv_ref[pr]).astype(o_ref.dtype)


def _dsa(proj3, qi_t, w_t, ki, ct, c_pad, wuk_bd, wuv_bd, bias_d, far):
    b, s, _ = proj3.shape
    nq = s // Q_BLOCK
    rows = DSA_HEADS * Q_BLOCK
    n_tiles = -(-nq // 4)
    const = dict(pipeline_mode=pl.Buffered(1))

    def whole(a):
        nd = a.ndim
        return pl.BlockSpec((1,) + a.shape[1:], lambda bi, i: (bi,) + (0,) * (nd - 1))

    return pl.pallas_call(
        _dsa_kernel,
        grid=(b, nq),
        in_specs=[
            pl.BlockSpec(memory_space=pltpu.SMEM),
            pl.BlockSpec((1, Q_BLOCK, DSA_HEADS * DSA_HEAD_DIM), lambda bi, i: (bi, i, COL_QD // 1024)),
            pl.BlockSpec((1, 1, IDX_DIM, rows), lambda bi, i: (bi, i, 0, 0)),
            pl.BlockSpec((1, 1, IDX_HEADS, Q_BLOCK), lambda bi, i: (bi, i, 0, 0)),
            whole(ki), whole(ct), whole(c_pad),
            pl.BlockSpec(wuk_bd.shape, lambda bi, i: (0, 0, 0), **const),
            pl.BlockSpec(wuv_bd.shape, lambda bi, i: (0, 0, 0), **const),
            pl.BlockSpec(bias_d.shape, lambda bi, i: (0, 0, 0), **const),
        ],
        out_specs=pl.BlockSpec((1, Q_BLOCK, DSA_HEADS * DSA_HEAD_DIM), lambda bi, i: (bi, i, 0)),
        out_shape=jax.ShapeDtypeStruct((b, s, DSA_HEADS * DSA_HEAD_DIM), MXU_DTYPE),
        scratch_shapes=[pltpu.VMEM((rows, DSA_KV_LATENT), MXU_DTYPE),
                        pltpu.VMEM((Q_BLOCK + 512 * n_tiles, Q_BLOCK), jnp.int32),
                        pltpu.VMEM((1 + 4 * n_tiles, Q_BLOCK, Q_BLOCK), jnp.int32),
                        pltpu.VMEM((rows, 128), jnp.float32),
                        pltpu.VMEM((rows, 128), jnp.float32),
                        pltpu.VMEM((rows, 128), jnp.float32),
                        pltpu.VMEM((rows, DSA_KV_LATENT), jnp.float32),
                        pltpu.VMEM((rows, 2 * Q_BLOCK), jnp.float32),
                        pltpu.VMEM((1, Q_BLOCK), jnp.int32)],
        compiler_params=pltpu.CompilerParams(
            dimension_semantics=("arbitrary", "arbitrary"), vmem_limit_bytes=VMEM_LIMIT_BYTES),
        name="dsa_attention",
    )(far, proj3, qi_t, w_t, ki, ct, c_pad, wuk_bd, wuv_bd, bias_d)


def _t5_bucket(dist):
    n = jnp.maximum(dist, 0)
    exact = REL_BUCKETS // 2
    nf = jnp.maximum(n, 1).astype(jnp.float32)
    large = exact + (jnp.log(nf / exact) / math.log(REL_MAX_DIST / exact)
                     * (REL_BUCKETS - exact)).astype(jnp.int32)
    large = jnp.minimum(large, REL_BUCKETS - 1)
    return jnp.where(n < exact, n, large)


def _bias_tables(rel_bias, nc):
    n_pos, n_neg = 1024, NSA_CMP_STRIDE * nc + 1024
    per_dist = rel_bias[_t5_bucket(jnp.arange(n_pos))]
    ext = jnp.concatenate([jnp.broadcast_to(per_dist[:1], (n_neg, per_dist.shape[1])), per_dist], axis=0)
    flipped = ext[::-1]

    def toeplitz(offset, count, stride, heads):
        rows = []
        for t_l in range(Q_BLOCK):
            start = n_pos - 1 - t_l - offset
            rows.append(lax.slice(flipped, (start, 0), (start + (count - 1) * stride + 1, flipped.shape[1]),
                                  (stride, 1)))
        return jnp.transpose(jnp.stack(rows)[..., heads], (2, 0, 1))

    nsa, dsa = slice(0, NSA_HEADS), slice(NSA_HEADS, NSA_HEADS + DSA_HEADS)
    bias_w = toeplitz(NSA_WINDOW, Q_BLOCK + NSA_WINDOW, 1, nsa)
    bias_d = toeplitz(Q_BLOCK, 2 * Q_BLOCK, 1, dsa)
    bias_c = toeplitz(225, nc, NSA_CMP_STRIDE, nsa)
    return bias_w, bias_c, bias_d, rel_bias[REL_BUCKETS - 1]


def _overlap_t(s):
    n_blk, nc = s // NSA_SEL_LEN, s // NSA_CMP_STRIDE
    cmp_start = np.arange(nc) * NSA_CMP_STRIDE
    sel_start = np.arange(n_blk) * NSA_SEL_LEN
    ov = (cmp_start[None, :] < sel_start[:, None] + NSA_SEL_LEN) & (cmp_start[None, :] + NSA_CMP_LEN > sel_start[:, None])
    return jnp.asarray(ov.astype(np.float32)).astype(MXU_DTYPE)


def _block_diag_pairs(w_uk, w_uv):
    hp = DSA_HEADS // 2
    dh, lat = DSA_HEAD_DIM, DSA_KV_LATENT
    uk = jnp.transpose(w_uk, (1, 2, 0)).reshape(hp, 2, dh, lat)
    uv = jnp.transpose(w_uv, (1, 0, 2)).reshape(hp, 2, lat, dh)
    z_k, z_v = jnp.zeros((hp, dh, lat), w_uk.dtype), jnp.zeros((hp, lat, dh), w_uv.dtype)
    wuk_bd = jnp.concatenate([jnp.concatenate([uk[:, 0], z_k], axis=2),
                              jnp.concatenate([z_k, uk[:, 1]], axis=2)], axis=1)
    wuv_bd = jnp.concatenate([jnp.concatenate([uv[:, 0], z_v], axis=2),
                              jnp.concatenate([z_v, uv[:, 1]], axis=2)], axis=1)
    return wuk_bd.astype(MXU_DTYPE), wuv_bd.astype(MXU_DTYPE)


def _mixers(proj, b, s, pe_k, pe_v, wk1, wk2, wv1, wv2, g_kv, w_uk, w_uv, tables, ovt):
    bias_w, bias_c, bias_d, far = tables
    p3 = proj.reshape(b, s, N_IN_PAD)
    nq, nblk = s // Q_BLOCK, s // Q_BLOCK
    g, dk = NSA_KV_GROUPS, NSA_HEAD_DIM

    def piece(off, width):
        return p3[:, :, off:off + width]

    def heads_major(a, heads, dim):
        return a.reshape(b, nq, Q_BLOCK, heads, dim).transpose(0, 1, 3, 2, 4).reshape(b, nq, heads * Q_BLOCK, dim)

    def chunked(a):
        return a.reshape(b, s // 16, 16, g, dk).transpose(0, 3, 1, 2, 4).reshape(b, g, s // 16, 16 * dk)

    def keys_t(a, pad_blocks):
        t = a.reshape(b, nblk, Q_BLOCK, a.shape[-1]).transpose(0, 1, 3, 2)
        return jnp.pad(t, ((0, 0), (pad_blocks, 0), (0, 0), (0, 0)))

    def rows_padded(a, pad):
        return jnp.pad(a, ((0, 0), (pad, 0), (0, 0)))

    kct = _compress(chunked(piece(COL_KC, 256)), pe_k, wk1, wk2, transposed=True).reshape(b, g * dk, s // 16)
    vcm = _compress(chunked(piece(COL_VC, 256)), pe_v, wv1, wv2, transposed=False)
    vcm = vcm.transpose(0, 2, 1, 3).reshape(b, s // 16, g * dk)
    o_n = _nsa(p3, heads_major(piece(COL_QN, 1024), NSA_HEADS, dk), kct, vcm,
               keys_t(piece(COL_KS, 256), 1), rows_padded(piece(COL_VS, 256), Q_BLOCK),
               keys_t(piece(COL_KW, 256), NSA_WINDOW // Q_BLOCK), rows_padded(piece(COL_VW, 256), NSA_WINDOW),
               bias_w, bias_c, ovt, far)

    ckv = _ckvnorm(proj, g_kv).reshape(b, s, DSA_KV_LATENT)
    ki = piece(COL_SMALL + 48, IDX_DIM)
    qi_t = piece(COL_QI, 512).reshape(b, nq, Q_BLOCK, IDX_HEADS, IDX_DIM).transpose(0, 1, 4, 3, 2)
    qi_t = qi_t.reshape(b, nq, IDX_DIM, IDX_HEADS * Q_BLOCK)
    w_t = piece(COL_SMALL + 80, IDX_HEADS).reshape(b, nq, Q_BLOCK, IDX_HEADS).transpose(0, 1, 3, 2)
    wuk_bd, wuv_bd = _block_diag_pairs(w_uk, w_uv)
    o_d = _dsa(p3, qi_t, w_t, ki, keys_t(ckv, 1), rows_padded(ckv, Q_BLOCK), wuk_bd, wuv_bd, bias_d, far)
    return o_n.reshape(b * s, -1), o_d.reshape(b * s, -1)


def _permute_w_in(w_in_l):
    d = w_in_l.shape[0]
    sizes = (1024, 256, 256, 256, 256, 256, 256, 48, 1024, 256, 512, 32, 16, 4096)
    offs = np.concatenate([[0], np.cumsum(sizes)])
    (q_n, kc, vc, ks, vs, kw, vw, g_n, q_d, ckv, qi, ki, wi, a_m) = [
        w_in_l[:, int(offs[k]):int(offs[k + 1])] for k in range(len(sizes))]
    small = jnp.concatenate([g_n, ki, wi, jnp.zeros((d, 32), w_in_l.dtype)], axis=1)
    out = jnp.concatenate([a_m, q_n, q_d, kc, vc, ks, vs, kw, vw, qi, ckv, small,
                           jnp.zeros((d, N_IN_PAD - 8576), w_in_l.dtype)], axis=1)
    return out.astype(MXU_DTYPE)


def kernel(x, c, w_ada, b_ada, g_norm, w_ffn_in, w_ffn_out, w_in, nsa_pe_k, nsa_pe_v, nsa_cmp_k1, nsa_cmp_k2,
           nsa_cmp_v1, nsa_cmp_v2, dsa_g_kv, dsa_w_uk, dsa_w_uv, w_up_nsa, w_up_dsa, w_out, rel_bias, g_final):
    B, S, D = x.shape
    N = B * S
    mod = _adaln(c, w_ada, b_ada)[:, :B].reshape(DEPTH, B, N_SUBLAYERS, 3, 1, D)
    tables = _bias_tables(rel_bias, S // NSA_CMP_STRIDE)
    ovt = _overlap_t(S)
    x2d = x.reshape(N, D)
    for l in range(DEPTH):
        m0, m1, m2 = mod[l, :, 0], mod[l, :, 1], mod[l, :, 2]
        x2d = _ffn(x2d, m0[:, 0], m0[:, 1], m0[:, 2], g_norm[l, 0],
                   w_ffn_in[l, 0].astype(MXU_DTYPE), w_ffn_out[l, 0].astype(MXU_DTYPE), g_final,
                   seq=S, final_norm=False)
        proj = _inproj(x2d, m1[:, 0], m1[:, 1], g_norm[l, 1], _permute_w_in(w_in[l]), seq=S)
        o_n, o_d = _mixers(proj, B, S, nsa_pe_k[l], nsa_pe_v[l], nsa_cmp_k1[l], nsa_cmp_k2[l], nsa_cmp_v1[l],
                           nsa_cmp_v2[l], dsa_g_kv[l], dsa_w_uk[l], dsa_w_uv[l], tables, ovt)
        x2d = _merge(x2d, m1[:, 2], o_n, o_d, proj, w_up_nsa[l].astype(MXU_DTYPE),
                     w_up_dsa[l].astype(MXU_DTYPE), w_out[l].astype(MXU_DTYPE), seq=S)
        x2d = _ffn(x2d, m2[:, 0], m2[:, 1], m2[:, 2], g_norm[l, 2],
                   w_ffn_in[l, 1].astype(MXU_DTYPE), w_ffn_out[l, 1].astype(MXU_DTYPE), g_final,
                   seq=S, final_norm=(l == DEPTH - 1))
    return x2d.reshape(B, S, D)
```

```python
import functools
import math

import jax
import jax.numpy as jnp
import numpy as np
from jax import lax
from jax.experimental import pallas as pl
from jax.experimental.pallas import tpu as pltpu

D_MODEL = 2048
DEPTH = 2
NSA_HEADS = 16
NSA_KV_GROUPS = 4
NSA_HEAD_DIM = 64
NSA_CMP_LEN = 32
NSA_CMP_STRIDE = 16
NSA_CMP_HIDDEN = 256
NSA_SEL_LEN = 64
NSA_SEL_BLOCKS = 16
NSA_WINDOW = 512
DSA_HEADS = 16
DSA_HEAD_DIM = 64
DSA_KV_LATENT = 256
IDX_HEADS = 16
IDX_DIM = 32
DSA_TOPK_MAX = 256
INDEX_SCALE = (IDX_HEADS * IDX_DIM) ** -0.5
D_FF = 5632
REL_BUCKETS = 32
REL_MAX_DIST = 128
Q_BLOCK = 128
N_SUBLAYERS = 3
RMS_EPS = 1e-6
NEG_INF = -1e30
FORCE_BONUS = 1e4
NSA_KV_WIDTH = NSA_KV_GROUPS * NSA_HEAD_DIM

MXU_DTYPE = jnp.bfloat16
VMEM_LIMIT_BYTES = 56 * 1024 * 1024

COL_AM = 0
COL_QN = 4096
COL_QD = 5120
COL_KC, COL_VC, COL_KS, COL_VS, COL_KW, COL_VW = 6144, 6400, 6656, 6912, 7168, 7424
COL_QI = 7680
COL_CKV = 8192
COL_SMALL = 8448
N_IN_PAD = 8704


def _dot(a, b):
    return jnp.dot(a, b, preferred_element_type=jnp.float32)


def _rms_scale(x):
    return x * lax.rsqrt(jnp.mean(x * x, axis=-1, keepdims=True) + RMS_EPS)


def _modulated_norm(x, g, shift, scale):
    return (_rms_scale(x) * g) * (1.0 + scale) + shift


def _split_hi_lo(x):
    hi = x.astype(MXU_DTYPE)
    return hi, (x - hi.astype(jnp.float32)).astype(MXU_DTYPE)


def _adaln_kernel(c_ref, w_ref, b_ref, o_ref):
    c = c_ref[...]
    c_hi, c_lo = _split_hi_lo(c * jax.nn.sigmoid(c))
    w_hi, w_lo = _split_hi_lo(w_ref[0])
    o_ref[0] = _dot(c_hi, w_hi) + _dot(c_hi, w_lo) + _dot(c_lo, w_hi) + b_ref[0]


def _adaln(c, w_ada, b_ada):
    depth, d, n = w_ada.shape
    rows = 16
    c = jnp.pad(c, ((0, rows - c.shape[0]), (0, 0)))
    b = rows
    tn = 1024
    return pl.pallas_call(
        _adaln_kernel,
        grid=(depth, n // tn),
        in_specs=[
            pl.BlockSpec((b, d), lambda l, j: (0, 0)),
            pl.BlockSpec((1, d, tn), lambda l, j: (l, 0, j)),
            pl.BlockSpec((1, 1, tn), lambda l, j: (l, 0, j)),
        ],
        out_specs=pl.BlockSpec((1, b, tn), lambda l, j: (l, 0, j)),
        out_shape=jax.ShapeDtypeStruct((depth, b, n), jnp.float32),
        compiler_params=pltpu.CompilerParams(
            dimension_semantics=("arbitrary", "arbitrary"), vmem_limit_bytes=VMEM_LIMIT_BYTES),
        name="adaln",
    )(c, w_ada, b_ada.reshape(depth, 1, n))


def _ffn_kernel(x_ref, shift_ref, scale_ref, gate_ref, g_ref, wg_ref, wu_ref, wo_ref, gfin_ref, o_ref,
                h_scr, acc_scr, *, final_norm):
    j = pl.program_id(1)

    @pl.when(j == 0)
    def _():
        h = _modulated_norm(x_ref[...], g_ref[...], shift_ref[0], scale_ref[0])
        h_scr[...] = h.astype(h_scr.dtype)
        acc_scr[...] = jnp.zeros_like(acc_scr)

    h = h_scr[...]
    g = _dot(h, wg_ref[...])
    u = _dot(h, wu_ref[...])
    a = (g * jax.nn.sigmoid(g)) * u
    acc_scr[...] += _dot(a.astype(wo_ref.dtype), wo_ref[...])

    @pl.when(j == pl.num_programs(1) - 1)
    def _():
        y = x_ref[...] + (0.5 * gate_ref[0]) * acc_scr[...]
        if final_norm:
            y = _rms_scale(y) * gfin_ref[...]
        o_ref[...] = y


def _ffn(x2d, shift, scale, gate, g, w_i, w_o, g_final, *, seq, final_norm):
    n, d = x2d.shape
    tm, tf = 512, 512
    nf = D_FF // tf
    per_b = seq // tm
    mod_spec = pl.BlockSpec((1, 1, d), lambda i, j: (i // per_b, 0, 0))
    return pl.pallas_call(
        functools.partial(_ffn_kernel, final_norm=final_norm),
        grid=(n // tm, nf),
        in_specs=[
            pl.BlockSpec((tm, d), lambda i, j: (i, 0)),
            mod_spec, mod_spec, mod_spec,
            pl.BlockSpec((1, d), lambda i, j: (0, 0)),
            pl.BlockSpec((d, tf), lambda i, j: (0, j)),
            pl.BlockSpec((d, tf), lambda i, j: (0, j + nf)),
            pl.BlockSpec((tf, d), lambda i, j: (j, 0)),
            pl.BlockSpec((1, d), lambda i, j: (0, 0)),
        ],
        out_specs=pl.BlockSpec((tm, d), lambda i, j: (i, 0)),
        out_shape=jax.ShapeDtypeStruct((n, d), jnp.float32),
        scratch_shapes=[pltpu.VMEM((tm, d), MXU_DTYPE), pltpu.VMEM((tm, d), jnp.float32)],
        compiler_params=pltpu.CompilerParams(
            dimension_semantics=("arbitrary", "arbitrary"), vmem_limit_bytes=VMEM_LIMIT_BYTES),
        name="ffn",
    )(x2d, shift, scale, gate, g.reshape(1, d), w_i, w_i, w_o, g_final.reshape(1, d))


def _inproj_kernel(x_ref, shift_ref, scale_ref, g_ref, w_ref, o_ref, h_scr):
    @pl.when(pl.program_id(1) == 0)
    def _():
        h = _modulated_norm(x_ref[...], g_ref[...], shift_ref[0], scale_ref[0])
        h_scr[...] = h.astype(h_scr.dtype)

    o_ref[...] = _dot(h_scr[...], w_ref[...]).astype(o_ref.dtype)


def _inproj(x2d, shift, scale, g, w, *, seq):
    n, d = x2d.shape
    tm, tn = 512, N_IN_PAD // 4
    per_b = seq // tm
    mod_spec = pl.BlockSpec((1, 1, d), lambda i, j: (i // per_b, 0, 0))
    return pl.pallas_call(
        _inproj_kernel,
        grid=(n // tm, N_IN_PAD // tn),
        in_specs=[
            pl.BlockSpec((tm, d), lambda i, j: (i, 0)),
            mod_spec, mod_spec,
            pl.BlockSpec((1, d), lambda i, j: (0, 0)),
            pl.BlockSpec((d, tn), lambda i, j: (0, j)),
        ],
        out_specs=pl.BlockSpec((tm, tn), lambda i, j: (i, j)),
        out_shape=jax.ShapeDtypeStruct((n, N_IN_PAD), MXU_DTYPE),
        scratch_shapes=[pltpu.VMEM((tm, d), MXU_DTYPE)],
        compiler_params=pltpu.CompilerParams(
            dimension_semantics=("arbitrary", "arbitrary"), vmem_limit_bytes=VMEM_LIMIT_BYTES),
        name="inproj",
    )(x2d, shift, scale, g.reshape(1, d), w)


def _merge_kernel(x_ref, gate_ref, on_ref, od_ref, a0_ref, a1_ref, wun_ref, wud_ref, wout_ref, o_ref):
    yn = _dot(on_ref[...], wun_ref[...])
    yd = _dot(od_ref[...], wud_ref[...])
    a0 = jax.nn.sigmoid(a0_ref[...].astype(jnp.float32))
    a1 = jax.nn.sigmoid(a1_ref[...].astype(jnp.float32))
    y = a0 * yn + a1 * yd
    o_ref[...] = x_ref[...] + gate_ref[0] * _dot(y.astype(wout_ref.dtype), wout_ref[...])


def _merge(x2d, gate, o_n, o_d, proj, w_up_nsa, w_up_dsa, w_out, *, seq):
    n, d = x2d.shape
    tm = 512
    per_b = seq // tm
    hw = o_n.shape[1]
    const = dict(pipeline_mode=pl.Buffered(1))
    return pl.pallas_call(
        _merge_kernel,
        grid=(n // tm,),
        in_specs=[
            pl.BlockSpec((tm, d), lambda i: (i, 0)),
            pl.BlockSpec((1, 1, d), lambda i: (i // per_b, 0, 0)),
            pl.BlockSpec((tm, hw), lambda i: (i, 0)),
            pl.BlockSpec((tm, hw), lambda i: (i, 0)),
            pl.BlockSpec((tm, d), lambda i: (i, COL_AM // d)),
            pl.BlockSpec((tm, d), lambda i: (i, COL_AM // d + 1)),
            pl.BlockSpec((hw, d), lambda i: (0, 0), **const),
            pl.BlockSpec((hw, d), lambda i: (0, 0), **const),
            pl.BlockSpec((d, d), lambda i: (0, 0), **const),
        ],
        out_specs=pl.BlockSpec((tm, d), lambda i: (i, 0)),
        out_shape=jax.ShapeDtypeStruct((n, d), jnp.float32),
        compiler_params=pltpu.CompilerParams(
            dimension_semantics=("arbitrary",), vmem_limit_bytes=VMEM_LIMIT_BYTES),
        name="merge",
    )(x2d, gate, o_n, o_d, proj, proj, w_up_nsa, w_up_dsa, w_out)


def _dot_nt(a, b):
    return lax.dot_general(a, b, (((1,), (1,)), ((), ())), preferred_element_type=jnp.float32)


def _iota(shape, dim):
    return lax.broadcasted_iota(jnp.int32, shape, dim)


def _softmax_rows(s, mask):
    s = jnp.where(mask, s, NEG_INF)
    m = jnp.max(s, axis=1, keepdims=True)
    p = jnp.where(mask, jnp.exp(s - m), 0.0)
    return p * (1.0 / jnp.maximum(jnp.sum(p, axis=1, keepdims=True), 1e-30))


def _lane_tiles(x):
    return [x[:, u * 128:(u + 1) * 128] for u in range(x.shape[1] // 128)]


def _max_pass(s_parts, mask, rows0, mx_scr):
    qb = s_parts[0].shape[0]
    for r, s in enumerate(s_parts):
        tiles = _lane_tiles(jnp.where(mask, s, NEG_INF))
        red = functools.reduce(jnp.maximum, tiles)
        sl = slice(rows0 + r * qb, rows0 + (r + 1) * qb)
        mx_scr[sl, :] = jnp.maximum(mx_scr[sl, :], red)


def _exp_pass(s_parts, mask, shifts, v_tile, rows0, ls_scr, acc_scr):
    qb = s_parts[0].shape[0]
    ps = []
    for r, (s, shift) in enumerate(zip(s_parts, shifts)):
        if mask is not None:
            s = jnp.where(mask, s, NEG_INF)
        ptiles = [jnp.exp(t - shift) for t in _lane_tiles(s)]
        sl = slice(rows0 + r * qb, rows0 + (r + 1) * qb)
        ls_scr[sl, :] = ls_scr[sl, :] + functools.reduce(jnp.add, ptiles)
        ps.append(jnp.concatenate(ptiles, axis=1))
    p = jnp.concatenate(ps, axis=0).astype(v_tile.dtype)
    rows = slice(rows0, rows0 + len(s_parts) * qb)
    acc_scr[rows, :] = acc_scr[rows, :] + _dot(p, v_tile)


def _compress_kernel(x_ref, pe_ref, w1_ref, w2_ref, o_ref, *, transposed):
    x = x_ref[0, 0]
    half = x.shape[1]
    first = _dot(x, w1_ref[0:half, :])
    second = _dot(x, w1_ref[half:2 * half, :])
    pe_term = _dot(pe_ref[...], w1_ref[...])[0:1]
    hid = first + pltpu.roll(second, x.shape[0] - 1, 0) + pe_term
    hid = (hid * jax.nn.sigmoid(hid)).astype(w2_ref.dtype)
    if transposed:
        o_ref[0, 0] = _dot_nt(w2_ref[...], hid).astype(o_ref.dtype)
    else:
        o_ref[0, 0] = _dot(hid, w2_ref[...]).astype(o_ref.dtype)


def _compress(x_chunks, pe, w1, w2, *, transposed):
    b, g, nc, width = x_chunks.shape
    dk = NSA_HEAD_DIM
    pe8 = jnp.zeros((8, 2 * width), MXU_DTYPE).at[0].set(pe.reshape(-1).astype(MXU_DTYPE))
    w2m = (w2.T if transposed else w2).astype(MXU_DTYPE)
    oshape = (b, g, dk, nc) if transposed else (b, g, nc, dk)
    return pl.pallas_call(
        functools.partial(_compress_kernel, transposed=transposed),
        grid=(b, g),
        in_specs=[
            pl.BlockSpec((1, 1, nc, width), lambda i, j: (i, j, 0, 0)),
            pl.BlockSpec((8, 2 * width), lambda i, j: (0, 0)),
            pl.BlockSpec((2 * width, NSA_CMP_HIDDEN), lambda i, j: (0, 0)),
            pl.BlockSpec(w2m.shape, lambda i, j: (0, 0)),
        ],
        out_specs=pl.BlockSpec((1, 1) + oshape[2:], lambda i, j: (i, j, 0, 0)),
        out_shape=jax.ShapeDtypeStruct(oshape, MXU_DTYPE),
        compiler_params=pltpu.CompilerParams(dimension_semantics=("arbitrary", "arbitrary")),
        name="nsa_compress",
    )(x_chunks, pe8, w1.astype(MXU_DTYPE), w2m)


def _ckvnorm_kernel(x_ref, g_ref, o_ref):
    x = x_ref[...].astype(jnp.float32)
    o_ref[...] = (_rms_scale(x) * g_ref[...]).astype(o_ref.dtype)


def _ckvnorm(proj, g_kv):
    n = proj.shape[0]
    tm = 2048
    return pl.pallas_call(
        _ckvnorm_kernel,
        grid=(n // tm,),
        in_specs=[pl.BlockSpec((tm, DSA_KV_LATENT), lambda i: (i, COL_CKV // DSA_KV_LATENT)),
                  pl.BlockSpec((1, DSA_KV_LATENT), lambda i: (0, 0))],
        out_specs=pl.BlockSpec((tm, DSA_KV_LATENT), lambda i: (i, 0)),
        out_shape=jax.ShapeDtypeStruct((n, DSA_KV_LATENT), MXU_DTYPE),
        compiler_params=pltpu.CompilerParams(dimension_semantics=("arbitrary",)),
        name="ckv_norm",
    )(proj, g_kv.reshape(1, -1).astype(jnp.float32))


def _nsa_kernel(far_ref, q_ref, small_ref, kct_ref, vc_ref, kst_ref, vs_ref, kwt_ref, vw_ref, bw_ref, bc_ref,
                ovt_ref, o_ref, qs_scr, sel_scr, mxf_scr, mb_scr, ls_scr, acc_scr, sn_scr, oc_scr, ow_scr):
    i = pl.program_id(1)
    t0 = i * Q_BLOCK
    qb, dk, hg = Q_BLOCK, NSA_HEAD_DIM, NSA_HEADS // NSA_KV_GROUPS
    grows = hg * qb
    nc = kct_ref.shape[2]
    nb = ovt_ref.shape[0]
    wk = qb + NSA_WINDOW
    f32 = jnp.float32

    gates = jax.nn.sigmoid(small_ref[0].astype(f32))
    n_io = _iota((qb, nc), 1)
    tc_io = t0 + _iota((qb, nc), 0)
    mask_c = NSA_CMP_STRIDE * n_io + (NSA_CMP_LEN - 1) <= tc_io
    near_lo = (qb // NSA_CMP_STRIDE) * i - 16
    near_c = n_io >= near_lo
    shift_c = jnp.where(near_lo >= 0, near_lo, near_lo + nc)
    j_io = _iota((nb, qb), 0)
    tj_io = t0 + _iota((nb, qb), 1)
    cur = jnp.right_shift(tj_io, 6)
    forced = (j_io == 0) | (j_io == cur) | (j_io == cur - 1)
    visible = j_io * NSA_SEL_LEN <= tj_io
    eye = jnp.where(_iota((qb, qb), 0) == _iota((qb, qb), 1), 1.0, 0.0).astype(MXU_DTYPE)
    kw_io = _iota((qb, wk), 1)
    off_w = _iota((qb, wk), 0) - kw_io + NSA_WINDOW
    mask_w = (off_w >= 0) & (off_w < NSA_WINDOW) & (t0 - NSA_WINDOW + kw_io >= 0)
    kn_io = _iota((qb, 2 * qb), 1)
    pos_n = t0 - qb + kn_io
    causal_n = (pos_n <= t0 + _iota((qb, 2 * qb), 0)) & (pos_n >= 0)
    n_far = (i + 2) // 4
    far_blocks = 2 * (i - 1)
    jsel_io = _iota((qb, nb), 1)
    qs_scr[...] = q_ref[0, 0] * (dk ** -0.5)

    for g in range(NSA_KV_GROUPS):
        rows = pl.ds(g * grows, grows)
        gsl = slice(g * dk, (g + 1) * dk)
        qg = qs_scr[g * grows:(g + 1) * grows, :]

        s = _dot(qg, kct_ref[0, gsl, :])
        probs = []
        for r in range(hg):
            h = g * hg + r
            bias = jnp.where(near_c, pltpu.roll(bc_ref[h], shift_c, 1), far_ref[h])
            probs.append(_softmax_rows(s[r * qb:(r + 1) * qb] + bias, mask_c))
        pstack = jnp.concatenate(probs, axis=0)
        oc_scr[rows, :] = _dot(pstack.astype(vc_ref.dtype), vc_ref[0])
        pg = probs[0] + probs[1] + probs[2] + probs[3]
        hi = pg.astype(MXU_DTYPE)
        r1 = pg - hi.astype(f32)
        mid = r1.astype(MXU_DTYPE)
        lo = (r1 - mid.astype(f32)).astype(MXU_DTYPE)
        ovt = ovt_ref[...]
        imp = _dot_nt(ovt, hi) + _dot_nt(ovt, mid) + _dot_nt(ovt, lo)
        imp = jnp.where(visible, imp + jnp.where(forced, FORCE_BONUS, 0.0), NEG_INF)
        ranks = [jnp.zeros((nb, qb), f32) for _ in range(4)]
        for jp in range(nb):
            row = imp[jp:jp + 1, :]
            beats = jnp.where(row > imp, 1.0, jnp.where(row == imp, jnp.where(jp < j_io, 1.0, 0.0), 0.0))
            ranks[jp % 4] = ranks[jp % 4] + beats
        rank = (ranks[0] + ranks[1]) + (ranks[2] + ranks[3])
        sel_t = jnp.where(rank < NSA_SEL_BLOCKS, 1.0, 0.0).astype(MXU_DTYPE)
        sel = _dot_nt(eye, sel_t)
        sel_scr[2 * g] = sel
        sel_scr[2 * g + 1] = jnp.where(jsel_io < far_blocks, sel, 0.0)

        kwt = jnp.concatenate([kwt_ref[0, i + u, gsl, :] for u in range(wk // qb)], axis=1)
        s = _dot(qg, kwt)
        probs, sums = [], []
        for r in range(hg):
            sh = jnp.where(mask_w, s[r * qb:(r + 1) * qb] + bw_ref[g * hg + r], NEG_INF)
            p = jnp.exp(sh - jnp.max(sh, axis=1, keepdims=True))
            probs.append(p)
            sums.append(jnp.sum(p, axis=1, keepdims=True))
        pstack = jnp.concatenate(probs, axis=0)
        ow = _dot(pstack.astype(vw_ref.dtype), vw_ref[0, pl.ds(pl.multiple_of(t0, qb), wk), :])
        for r in range(hg):
            ow_scr[g * grows + r * qb:g * grows + (r + 1) * qb, :] = ow[r * qb:(r + 1) * qb] * (1.0 / sums[r])

    def far_tile(jt, g):
        gsl = slice(g * dk, (g + 1) * dk)
        kt = jnp.concatenate([kst_ref[0, 1 + 4 * jt + u, gsl, :] for u in range(4)], axis=1)
        expand = jnp.where(_iota((nb, 512), 0) == 8 * jt + jnp.right_shift(_iota((nb, 512), 1), 6),
                           1.0, 0.0).astype(MXU_DTYPE)
        mask = _dot(sel_scr[2 * g + 1].astype(MXU_DTYPE), expand) > 0.5
        sc = _dot(qs_scr[g * grows:(g + 1) * grows, :], kt)
        return [sc[r * qb:(r + 1) * qb] for r in range(hg)], mask

    mxf_scr[...] = jnp.full(mxf_scr.shape, NEG_INF, f32)
    ls_scr[...] = jnp.zeros(ls_scr.shape, f32)
    acc_scr[...] = jnp.zeros(acc_scr.shape, f32)

    def far_max_body(jt, carry):
        for g in range(NSA_KV_GROUPS):
            parts, mask = far_tile(jt, g)
            _max_pass(parts, mask, g * grows, mxf_scr)
        return carry

    lax.fori_loop(0, n_far, far_max_body, 0)

    expand_n = jnp.where(_iota((nb, 2 * qb), 0) == 2 * (i - 1) + jnp.right_shift(_iota((nb, 2 * qb), 1), 6),
                         1.0, 0.0).astype(MXU_DTYPE)
    for g in range(NSA_KV_GROUPS):
        gsl = slice(g * dk, (g + 1) * dk)
        kt = jnp.concatenate([kst_ref[0, i + u, gsl, :] for u in range(2)], axis=1)
        mask = (_dot(sel_scr[2 * g].astype(MXU_DTYPE), expand_n) > 0.5) & causal_n
        sc = _dot(qs_scr[g * grows:(g + 1) * grows, :], kt)
        for r in range(hg):
            h = g * hg + r
            hr = slice(g * grows + r * qb, g * grows + (r + 1) * qb)
            sn = jnp.where(mask, sc[r * qb:(r + 1) * qb] + bw_ref[h][:, NSA_WINDOW - qb:], NEG_INF)
            sn_scr[hr, :] = sn
            m_far = jnp.max(mxf_scr[hr, :], axis=1, keepdims=True) + far_ref[h]
            m_row = jnp.maximum(m_far, jnp.max(sn, axis=1, keepdims=True))
            mb_scr[hr, :] = jnp.broadcast_to(m_row, (qb, 128))

    def far_exp_body(jt, carry):
        vt = vs_ref[0, pl.ds(pl.multiple_of(qb + 512 * jt, qb), 512), :]
        for g in range(NSA_KV_GROUPS):
            parts, mask = far_tile(jt, g)
            shifts = [mb_scr[g * grows + r * qb:g * grows + (r + 1) * qb, :] - far_ref[g * hg + r]
                      for r in range(hg)]
            _exp_pass(parts, mask, shifts, vt, g * grows, ls_scr, acc_scr)
        return carry

    lax.fori_loop(0, n_far, far_exp_body, 0)

    vt_n = vs_ref[0, pl.ds(pl.multiple_of(t0, qb), 2 * qb), :]
    for g in range(NSA_KV_GROUPS):
        hrs = [slice(g * grows + r * qb, g * grows + (r + 1) * qb) for r in range(hg)]
        _exp_pass([sn_scr[hr, :] for hr in hrs], None, [mb_scr[hr, :] for hr in hrs], vt_n, g * grows,
                  ls_scr, acc_scr)

    for g in range(NSA_KV_GROUPS):
        slab = slice((g // 2) * 128, (g // 2) * 128 + 128)
        placed = []
        for r in range(hg):
            h = g * hg + r
            hr = slice(g * grows + r * qb, g * grows + (r + 1) * qb)
            o_s = acc_scr[hr, slab] * (1.0 / jnp.sum(ls_scr[hr, :], axis=1, keepdims=True))
            o_h = (gates[:, 3 * h:3 * h + 1] * oc_scr[hr, slab] + gates[:, 3 * h + 1:3 * h + 2] * o_s
                   + gates[:, 3 * h + 2:3 * h + 3] * ow_scr[hr, slab])
            if g % 2 != r % 2:
                o_h = pltpu.roll(o_h, dk, 1)
            placed.append(o_h)
        low_half = _iota((qb, 128), 1) < dk
        for pr in range(hg // 2):
            pair = jnp.where(low_half, placed[2 * pr], placed[2 * pr + 1])
            col = (g * hg // 2 + pr) * 128
            o_ref[0, :, col:col + 128] = pair.astype(o_ref.dtype)


def _nsa(proj3, q_r, kct, vc, kst, vs, kwt, vw, bias_w, bias_c, ovt, far):
    b, s, _ = proj3.shape
    nq = s // Q_BLOCK
    rows = NSA_HEADS * Q_BLOCK
    vdim = NSA_KV_WIDTH
    const = dict(pipeline_mode=pl.Buffered(1))

    def whole(a):
        nd = a.ndim
        return pl.BlockSpec((1,) + a.shape[1:], lambda bi, i: (bi,) + (0,) * (nd - 1))

    return pl.pallas_call(
        _nsa_kernel,
        grid=(b, nq),
        in_specs=[
            pl.BlockSpec(memory_space=pltpu.SMEM),
            pl.BlockSpec((1, 1, rows, NSA_HEAD_DIM), lambda bi, i: (bi, i, 0, 0)),
            pl.BlockSpec((1, Q_BLOCK, 128), lambda bi, i: (bi, i, COL_SMALL // 128)),
            whole(kct), whole(vc), whole(kst), whole(vs), whole(kwt), whole(vw),
            pl.BlockSpec(bias_w.shape, lambda bi, i: (0, 0, 0), **const),
            pl.BlockSpec(bias_c.shape, lambda bi, i: (0, 0, 0), **const),
            pl.BlockSpec(ovt.shape, lambda bi, i: (0, 0), **const),
        ],
        out_specs=pl.BlockSpec((1, Q_BLOCK, NSA_HEADS * NSA_HEAD_DIM), lambda bi, i: (bi, i, 0)),
        out_shape=jax.ShapeDtypeStruct((b, s, NSA_HEADS * NSA_HEAD_DIM), MXU_DTYPE),
        scratch_shapes=[pltpu.VMEM((rows, NSA_HEAD_DIM), MXU_DTYPE),
                        pltpu.VMEM((2 * NSA_KV_GROUPS, Q_BLOCK, ovt.shape[0]), jnp.float32),
                        pltpu.VMEM((rows, 128), jnp.float32),
                        pltpu.VMEM((rows, 128), jnp.float32),
                        pltpu.VMEM((rows, 128), jnp.float32),
                        pltpu.VMEM((rows, vdim), jnp.float32),
                        pltpu.VMEM((rows, 2 * Q_BLOCK), jnp.float32),
                        pltpu.VMEM((rows, vdim), jnp.float32),
                        pltpu.VMEM((rows, vdim), jnp.float32)],
        compiler_params=pltpu.CompilerParams(
            dimension_semantics=("arbitrary", "arbitrary"), vmem_limit_bytes=VMEM_LIMIT_BYTES),
        name="nsa_attention",
    )(far, q_r, proj3, kct, vc, kst, vs, kwt, vw, bias_w, bias_c, ovt)


INT_MIN = -2 ** 31


def _sortable_key(x):
    bits = pltpu.bitcast(x, jnp.int32)
    return bits ^ (jnp.right_shift(bits, 31) & 0x7FFFFFFF)


def _dsa_kernel(far_ref, qd_ref, qit_ref, wt_ref, ki_ref, ct_ref, c_ref, wuk_ref, wuv_ref, bd_ref, o_ref,
                qlat_scr, keyt_scr, key_scr, mxf_scr, mb_scr, ls_scr, acc_scr, sn_scr, tie_scr):
    i = pl.program_id(1)
    t0 = i * Q_BLOCK
    qb = Q_BLOCK
    f32 = jnp.float32
    k_top = float(DSA_TOPK_MAX)
    chunk = 4 * qb
    n_chunks = DSA_HEADS // 4

    for pr in range(DSA_HEADS // 2):
        r = _dot(qd_ref[0, :, pr * 128:(pr + 1) * 128], wuk_ref[pr]) * (DSA_HEAD_DIM ** -0.5)
        qlat_scr[2 * pr * qb:(2 * pr + 1) * qb, :] = r[:, :DSA_KV_LATENT].astype(qlat_scr.dtype)
        qlat_scr[(2 * pr + 1) * qb:(2 * pr + 2) * qb, :] = r[:, DSA_KV_LATENT:].astype(qlat_scr.dtype)

    n_tiles = i // 4 + 1
    w_t = wt_ref[0, 0].astype(f32) * INDEX_SCALE
    tq_io = t0 + _iota((512, qb), 1)
    keyt_scr[0:qb, :] = jnp.full((qb, qb), INT_MIN, jnp.int32)

    def tile_rows(jt):
        return pl.ds(pl.multiple_of(qb + 512 * jt, qb), 512)

    def index_body(jt, carry):
        kt = ki_ref[0, pl.ds(pl.multiple_of(512 * jt, 512), 512), :]
        sc = jnp.zeros((512, qb), f32)
        for hc in range(n_chunks):
            lg = _dot(kt, qit_ref[0, 0, :, hc * chunk:(hc + 1) * chunk])
            for r in range(4):
                h = hc * 4 + r
                sc = sc + jnp.maximum(lg[:, r * qb:(r + 1) * qb], 0.0) * w_t[h:h + 1, :]
        sc = jnp.where(512 * jt + _iota((512, qb), 0) <= tq_io, sc, NEG_INF)
        keyt_scr[tile_rows(jt), :] = _sortable_key(sc)
        return carry

    lax.fori_loop(0, n_tiles, index_body, 0)

    def count(pred):
        def body(jt, c):
            x = jnp.where(pred(keyt_scr[tile_rows(jt), :], 512 * jt + _iota((512, qb), 0)), 1.0, 0.0)
            parts = [x[8 * v:8 * v + 8] for v in range(512 // 8)]
            while len(parts) > 1:
                parts = [a + b for a, b in zip(parts[0::2], parts[1::2])]
            return c + parts[0]
        c = lax.fori_loop(0, n_tiles, body, jnp.zeros((8, qb), f32))
        return jnp.sum(c, axis=0, keepdims=True)

    thr = jnp.where(count(lambda kk, pos: kk >= 0) >= k_top, 0, INT_MIN).astype(jnp.int32)
    thr = thr + jnp.zeros((1, qb), jnp.int32)

    def bit_body(bit, thr):
        cand = thr | jnp.left_shift(jnp.int32(1), 30 - bit)
        return jnp.where(count(lambda kk, pos: kk >= cand) >= k_top, cand, thr)

    thr = lax.fori_loop(0, 31, bit_body, thr)
    need = k_top - count(lambda kk, pos: kk > thr)
    n_ge = count(lambda kk, pos: kk >= thr)
    tie_scr[...] = jnp.full((1, qb), 2 ** 30, jnp.int32)
    has_excess = jnp.max(jnp.where(n_ge > k_top, 1.0, 0.0)) > 0.5

    @pl.when(has_excess)
    def _():
        def pos_body(bit, last):
            cand = last | jnp.left_shift(jnp.int32(1), 12 - bit)
            n_before = count(lambda kk, pos: (kk == thr) & (pos < cand))
            return jnp.where(n_before < need, cand, last)
        tie_scr[...] = lax.fori_loop(0, 13, pos_body, jnp.zeros((1, qb), jnp.int32))

    last_tie = tie_scr[...]
    eye = jnp.where(_iota((qb, qb), 0) == _iota((qb, qb), 1), 1.0, 0.0).astype(MXU_DTYPE)

    def selected_rows(kk, pos, extra=None):
        chosen = (kk > thr) | ((kk == thr) & (pos <= last_tie))
        if extra is not None:
            chosen = chosen & extra
        return _dot_nt(eye, jnp.where(chosen, 1.0, 0.0).astype(MXU_DTYPE)) > 0.5

    pos_n = t0 - qb + _iota((2 * qb, qb), 0)
    causal_n = (pos_n <= t0 + _iota((2 * qb, qb), 1)) & (pos_n >= 0)
    mask_n = selected_rows(keyt_scr[pl.ds(pl.multiple_of(t0, qb), 2 * qb), :], pos_n, causal_n)

    def mask_body(jt, carry):
        mt = selected_rows(keyt_scr[tile_rows(jt), :], 512 * jt + _iota((512, qb), 0))
        for u in range(4):
            bi = 4 * jt + u
            key_scr[1 + bi] = jnp.where(mt[:, u * qb:(u + 1) * qb], jnp.where(bi < i - 1, 1, 0), 0)
        return carry

    lax.fori_loop(0, n_tiles, mask_body, 0)

    mxf_scr[...] = jnp.full(mxf_scr.shape, NEG_INF, f32)
    ls_scr[...] = jnp.zeros(ls_scr.shape, f32)
    acc_scr[...] = jnp.zeros(acc_scr.shape, f32)
    n_far = (i + 2) // 4

    def far_tile(jt, hc):
        ct = jnp.concatenate([ct_ref[0, 1 + 4 * jt + u] for u in range(4)], axis=1)
        sc = _dot(qlat_scr[hc * chunk:(hc + 1) * chunk, :], ct)
        return [sc[r * qb:(r + 1) * qb] for r in range(4)]

    def far_mask(jt):
        return jnp.concatenate([key_scr[1 + 4 * jt + u] for u in range(4)], axis=1) > 0

    def far_max_body(jt, carry):
        mask = far_mask(jt)
        for hc in range(n_chunks):
            _max_pass(far_tile(jt, hc), mask, hc * chunk, mxf_scr)
        return carry

    lax.fori_loop(0, n_far, far_max_body, 0)

    ct_n = jnp.concatenate([ct_ref[0, i], ct_ref[0, i + 1]], axis=1)
    for hc in range(n_chunks):
        sc = _dot(qlat_scr[hc * chunk:(hc + 1) * chunk, :], ct_n)
        for r in range(4):
            h = hc * 4 + r
            hr = slice(h * qb, (h + 1) * qb)
            sn = jnp.where(mask_n, sc[r * qb:(r + 1) * qb] + bd_ref[h], NEG_INF)
            sn_scr[hr, :] = sn
            m_far = jnp.max(mxf_scr[hr, :], axis=1, keepdims=True) + far_ref[NSA_HEADS + h]
            m_row = jnp.maximum(m_far, jnp.max(sn, axis=1, keepdims=True))
            mb_scr[hr, :] = jnp.broadcast_to(m_row, (qb, 128))

    def far_exp_body(jt, carry):
        mask = far_mask(jt)
        cv = c_ref[0, pl.ds(pl.multiple_of(qb + 512 * jt, qb), 512), :]
        for hc in range(n_chunks):
            shifts = [mb_scr[(hc * 4 + r) * qb:(hc * 4 + r + 1) * qb, :] - far_ref[NSA_HEADS + hc * 4 + r]
                      for r in range(4)]
            _exp_pass(far_tile(jt, hc), mask, shifts, cv, hc * chunk, ls_scr, acc_scr)
        return carry

    lax.fori_loop(0, n_far, far_exp_body, 0)

    cv_n = c_ref[0, pl.ds(pl.multiple_of(t0, qb), 2 * qb), :]
    for hc in range(n_chunks):
        hrs = [slice((hc * 4 + r) * qb, (hc * 4 + r + 1) * qb) for r in range(4)]
        _exp_pass([sn_scr[hr, :] for hr in hrs], None, [mb_scr[hr, :] for hr in hrs], cv_n, hc * chunk,
                  ls_scr, acc_scr)

    def normalised(hr):
        return acc_scr[hr, :] * (1.0 / jnp.sum(ls_scr[hr, :], axis=1, keepdims=True))

    for pr in range(DSA_HEADS // 2):
        ra = slice(2 * pr * qb, (2 * pr + 1) * qb)
        rb = slice((2 * pr + 1) * qb, (2 * pr + 2) * qb)
        o_lat = jnp.concatenate([normalised(ra), normalised(rb)], axis=1)
        o_ref[0, :, pr * 128:(pr + 1) * 128] = _dot(o_lat.astype(wuv_ref.dtype), wuv_ref[pr]).astype(o_ref.dtype)


def _dsa(proj3, qi_t, w_t, ki, ct, c_pad, wuk_bd, wuv_bd, bias_d, far):
    b, s, _ = proj3.shape
    nq = s // Q_BLOCK
    rows = DSA_HEADS * Q_BLOCK
    n_tiles = -(-nq // 4)
    const = dict(pipeline_mode=pl.Buffered(1))

    def whole(a):
        nd = a.ndim
        return pl.BlockSpec((1,) + a.shape[1:], lambda bi, i: (bi,) + (0,) * (nd - 1))

    return pl.pallas_call(
        _dsa_kernel,
        grid=(b, nq),
        in_specs=[
            pl.BlockSpec(memory_space=pltpu.SMEM),
            pl.BlockSpec((1, Q_BLOCK, DSA_HEADS * DSA_HEAD_DIM), lambda bi, i: (bi, i, COL_QD // 1024)),
            pl.BlockSpec((1, 1, IDX_DIM, rows), lambda bi, i: (bi, i, 0, 0)),
            pl.BlockSpec((1, 1, IDX_HEADS, Q_BLOCK), lambda bi, i: (bi, i, 0, 0)),
            whole(ki), whole(ct), whole(c_pad),
            pl.BlockSpec(wuk_bd.shape, lambda bi, i: (0, 0, 0), **const),
            pl.BlockSpec(wuv_bd.shape, lambda bi, i: (0, 0, 0), **const),
            pl.BlockSpec(bias_d.shape, lambda bi, i: (0, 0, 0), **const),
        ],
        out_specs=pl.BlockSpec((1, Q_BLOCK, DSA_HEADS * DSA_HEAD_DIM), lambda bi, i: (bi, i, 0)),
        out_shape=jax.ShapeDtypeStruct((b, s, DSA_HEADS * DSA_HEAD_DIM), MXU_DTYPE),
        scratch_shapes=[pltpu.VMEM((rows, DSA_KV_LATENT), MXU_DTYPE),
                        pltpu.VMEM((Q_BLOCK + 512 * n_tiles, Q_BLOCK), jnp.int32),
                        pltpu.VMEM((1 + 4 * n_tiles, Q_BLOCK, Q_BLOCK), jnp.int32),
                        pltpu.VMEM((rows, 128), jnp.float32),
                        pltpu.VMEM((rows, 128), jnp.float32),
                        pltpu.VMEM((rows, 128), jnp.float32),
                        pltpu.VMEM((rows, DSA_KV_LATENT), jnp.float32),
                        pltpu.VMEM((rows, 2 * Q_BLOCK), jnp.float32),
                        pltpu.VMEM((1, Q_BLOCK), jnp.int32)],
        compiler_params=pltpu.CompilerParams(
            dimension_semantics=("arbitrary", "arbitrary"), vmem_limit_bytes=VMEM_LIMIT_BYTES),
        name="dsa_attention",
    )(far, proj3, qi_t, w_t, ki, ct, c_pad, wuk_bd, wuv_bd, bias_d)


def _t5_bucket(dist):
    n = jnp.maximum(dist, 0)
    exact = REL_BUCKETS // 2
    nf = jnp.maximum(n, 1).astype(jnp.float32)
    large = exact + (jnp.log(nf / exact) / math.log(REL_MAX_DIST / exact)
                     * (REL_BUCKETS - exact)).astype(jnp.int32)
    large = jnp.minimum(large, REL_BUCKETS - 1)
    return jnp.where(n < exact, n, large)


def _bias_tables(rel_bias, nc):
    n_pos, n_neg = 1024, NSA_CMP_STRIDE * nc + 1024
    per_dist = rel_bias[_t5_bucket(jnp.arange(n_pos))]
    ext = jnp.concatenate([jnp.broadcast_to(per_dist[:1], (n_neg, per_dist.shape[1])), per_dist], axis=0)
    flipped = ext[::-1].T

    def skew(vec, rows):
        m = vec.shape[-1]
        tiled = jnp.tile(vec, (1,) * (vec.ndim - 1) + (rows,))
        return tiled[..., :rows * (m - 1)].reshape(vec.shape[:-1] + (rows, m - 1))

    def toeplitz(offset, count, heads):
        start = n_pos - 1 - offset - (Q_BLOCK - 1)
        seg = flipped[heads, start:start + count + Q_BLOCK]
        return skew(seg, Q_BLOCK)[..., Q_BLOCK - 1:Q_BLOCK - 1 + count]

    nsa, dsa = slice(0, NSA_HEADS), slice(NSA_HEADS, NSA_HEADS + DSA_HEADS)
    bias_w = toeplitz(NSA_WINDOW, Q_BLOCK + NSA_WINDOW, nsa)
    bias_d = toeplitz(Q_BLOCK, 2 * Q_BLOCK, dsa)
    st, qa = NSA_CMP_STRIDE, Q_BLOCK // NSA_CMP_STRIDE
    base = n_pos - 1 - 225 - (st - 1) - st * (qa - 1)
    seg = flipped[nsa, base:base + st * (nc + qa)].reshape(NSA_HEADS, nc + qa, st)
    per_b = jnp.transpose(seg[..., ::-1], (0, 2, 1))
    bias_c = skew(per_b, qa)[..., qa - 1:qa - 1 + nc]
    bias_c = jnp.transpose(bias_c, (0, 2, 1, 3)).reshape(NSA_HEADS, Q_BLOCK, nc)
    return bias_w, bias_c, bias_d, rel_bias[REL_BUCKETS - 1]


def _overlap_t(s):
    n_blk, nc = s // NSA_SEL_LEN, s // NSA_CMP_STRIDE
    cmp_start = np.arange(nc) * NSA_CMP_STRIDE
    sel_start = np.arange(n_blk) * NSA_SEL_LEN
    ov = (cmp_start[None, :] < sel_start[:, None] + NSA_SEL_LEN) & (cmp_start[None, :] + NSA_CMP_LEN > sel_start[:, None])
    return jnp.asarray(ov.astype(np.float32)).astype(MXU_DTYPE)


def _block_diag_pairs(w_uk, w_uv):
    hp = DSA_HEADS // 2
    dh, lat = DSA_HEAD_DIM, DSA_KV_LATENT
    uk = jnp.transpose(w_uk, (1, 2, 0)).reshape(hp, 2, dh, lat)
    uv = jnp.transpose(w_uv, (1, 0, 2)).reshape(hp, 2, lat, dh)
    z_k, z_v = jnp.zeros((hp, dh, lat), w_uk.dtype), jnp.zeros((hp, lat, dh), w_uv.dtype)
    wuk_bd = jnp.concatenate([jnp.concatenate([uk[:, 0], z_k], axis=2),
                              jnp.concatenate([z_k, uk[:, 1]], axis=2)], axis=1)
    wuv_bd = jnp.concatenate([jnp.concatenate([uv[:, 0], z_v], axis=2),
                              jnp.concatenate([z_v, uv[:, 1]], axis=2)], axis=1)
    return wuk_bd.astype(MXU_DTYPE), wuv_bd.astype(MXU_DTYPE)


def _mixers(proj, b, s, pe_k, pe_v, wk1, wk2, wv1, wv2, g_kv, w_uk, w_uv, tables, ovt):
    bias_w, bias_c, bias_d, far = tables
    p3 = proj.reshape(b, s, N_IN_PAD)
    nq, nblk = s // Q_BLOCK, s // Q_BLOCK
    g, dk = NSA_KV_GROUPS, NSA_HEAD_DIM

    def piece(off, width):
        return p3[:, :, off:off + width]

    def heads_major(a, heads, dim):
        return a.reshape(b, nq, Q_BLOCK, heads, dim).transpose(0, 1, 3, 2, 4).reshape(b, nq, heads * Q_BLOCK, dim)

    def chunked(a):
        return a.reshape(b, s // 16, 16, g, dk).transpose(0, 3, 1, 2, 4).reshape(b, g, s // 16, 16 * dk)

    def keys_t(a, pad_blocks):
        t = a.reshape(b, nblk, Q_BLOCK, a.shape[-1]).transpose(0, 1, 3, 2)
        return jnp.pad(t, ((0, 0), (pad_blocks, 0), (0, 0), (0, 0)))

    def rows_padded(a, pad):
        return jnp.pad(a, ((0, 0), (pad, 0), (0, 0)))

    kct = _compress(chunked(piece(COL_KC, 256)), pe_k, wk1, wk2, transposed=True).reshape(b, g * dk, s // 16)
    vcm = _compress(chunked(piece(COL_VC, 256)), pe_v, wv1, wv2, transposed=False)
    vcm = vcm.transpose(0, 2, 1, 3).reshape(b, s // 16, g * dk)
    o_n = _nsa(p3, heads_major(piece(COL_QN, 1024), NSA_HEADS, dk), kct, vcm,
               keys_t(piece(COL_KS, 256), 1), rows_padded(piece(COL_VS, 256), Q_BLOCK),
               keys_t(piece(COL_KW, 256), NSA_WINDOW // Q_BLOCK), rows_padded(piece(COL_VW, 256), NSA_WINDOW),
               bias_w, bias_c, ovt, far)

    ckv = _ckvnorm(proj, g_kv).reshape(b, s, DSA_KV_LATENT)
    ki = piece(COL_SMALL + 48, IDX_DIM)
    qi_t = piece(COL_QI, 512).reshape(b, nq, Q_BLOCK, IDX_HEADS, IDX_DIM).transpose(0, 1, 4, 3, 2)
    qi_t = qi_t.reshape(b, nq, IDX_DIM, IDX_HEADS * Q_BLOCK)
    w_t = piece(COL_SMALL + 80, IDX_HEADS).reshape(b, nq, Q_BLOCK, IDX_HEADS).transpose(0, 1, 3, 2)
    wuk_bd, wuv_bd = _block_diag_pairs(w_uk, w_uv)
    o_d = _dsa(p3, qi_t, w_t, ki, keys_t(ckv, 1), rows_padded(ckv, Q_BLOCK), wuk_bd, wuv_bd, bias_d, far)
    return o_n.reshape(b * s, -1), o_d.reshape(b * s, -1)


def _permute_w_in(w_in_l):
    d = w_in_l.shape[0]
    sizes = (1024, 256, 256, 256, 256, 256, 256, 48, 1024, 256, 512, 32, 16, 4096)
    offs = np.concatenate([[0], np.cumsum(sizes)])
    (q_n, kc, vc, ks, vs, kw, vw, g_n, q_d, ckv, qi, ki, wi, a_m) = [
        w_in_l[:, int(offs[k]):int(offs[k + 1])] for k in range(len(sizes))]
    small = jnp.concatenate([g_n, ki, wi, jnp.zeros((d, 32), w_in_l.dtype)], axis=1)
    out = jnp.concatenate([a_m, q_n, q_d, kc, vc, ks, vs, kw, vw, qi, ckv, small,
                           jnp.zeros((d, N_IN_PAD - 8576), w_in_l.dtype)], axis=1)
    return out.astype(MXU_DTYPE)


def kernel(x, c, w_ada, b_ada, g_norm, w_ffn_in, w_ffn_out, w_in, nsa_pe_k, nsa_pe_v, nsa_cmp_k1, nsa_cmp_k2,
           nsa_cmp_v1, nsa_cmp_v2, dsa_g_kv, dsa_w_uk, dsa_w_uv, w_up_nsa, w_up_dsa, w_out, rel_bias, g_final):
    B, S, D = x.shape
    N = B * S
    mod = _adaln(c, w_ada, b_ada)[:, :B].reshape(DEPTH, B, N_SUBLAYERS, 3, 1, D)
    tables = _bias_tables(rel_bias, S // NSA_CMP_STRIDE)
    ovt = _overlap_t(S)
    x2d = x.reshape(N, D)
    for l in range(DEPTH):
        m0, m1, m2 = mod[l, :, 0], mod[l, :, 1], mod[l, :, 2]
        x2d = _ffn(x2d, m0[:, 0], m0[:, 1], m0[:, 2], g_norm[l, 0],
                   w_ffn_in[l, 0].astype(MXU_DTYPE), w_ffn_out[l, 0].astype(MXU_DTYPE), g_final,
                   seq=S, final_norm=False)
        proj = _inproj(x2d, m1[:, 0], m1[:, 1], g_norm[l, 1], _permute_w_in(w_in[l]), seq=S)
        o_n, o_d = _mixers(proj, B, S, nsa_pe_k[l], nsa_pe_v[l], nsa_cmp_k1[l], nsa_cmp_k2[l], nsa_cmp_v1[l],
                           nsa_cmp_v2[l], dsa_g_kv[l], dsa_w_uk[l], dsa_w_uv[l], tables, ovt)
        x2d = _merge(x2d, m1[:, 2], o_n, o_d, proj, w_up_nsa[l].astype(MXU_DTYPE),
                     w_up_dsa[l].astype(MXU_DTYPE), w_out[l].astype(MXU_DTYPE), seq=S)
        x2d = _ffn(x2d, m2[:, 0], m2[:, 1], m2[:, 2], g_norm[l, 2],
                   w_ffn_in[l, 1].astype(MXU_DTYPE), w_ffn_out[l, 1].astype(MXU_DTYPE), g_final,
                   seq=S, final_norm=(l == DEPTH - 1))
    return x2d.reshape(B, S, D)
```

```python
import functools
import math

import jax
import jax.numpy as jnp
import numpy as np
from jax import lax
from jax.experimental import pallas as pl
from jax.experimental.pallas import tpu as pltpu

D_MODEL = 2048
DEPTH = 2
NSA_HEADS = 16
NSA_KV_GROUPS = 4
NSA_HEAD_DIM = 64
NSA_CMP_LEN = 32
NSA_CMP_STRIDE = 16
NSA_CMP_HIDDEN = 256
NSA_SEL_LEN = 64
NSA_SEL_BLOCKS = 16
NSA_WINDOW = 512
DSA_HEADS = 16
DSA_HEAD_DIM = 64
DSA_KV_LATENT = 256
IDX_HEADS = 16
IDX_DIM = 32
DSA_TOPK_MAX = 256
INDEX_SCALE = (IDX_HEADS * IDX_DIM) ** -0.5
D_FF = 5632
REL_BUCKETS = 32
REL_MAX_DIST = 128
Q_BLOCK = 128
N_SUBLAYERS = 3
RMS_EPS = 1e-6
NEG_INF = -1e30
FORCE_BONUS = 1e4
NSA_KV_WIDTH = NSA_KV_GROUPS * NSA_HEAD_DIM

MXU_DTYPE = jnp.bfloat16
VMEM_LIMIT_BYTES = 56 * 1024 * 1024

COL_AM = 0
COL_QN = 4096
COL_QD = 5120
COL_KC, COL_VC, COL_KS, COL_VS, COL_KW, COL_VW = 6144, 6400, 6656, 6912, 7168, 7424
COL_QI = 7680
COL_CKV = 8192
COL_SMALL = 8448
N_IN_PAD = 8704


def _dot(a, b):
    return jnp.dot(a, b, preferred_element_type=jnp.float32)


def _rms_scale(x):
    return x * lax.rsqrt(jnp.mean(x * x, axis=-1, keepdims=True) + RMS_EPS)


def _modulated_norm(x, g, shift, scale):
    return (_rms_scale(x) * g) * (1.0 + scale) + shift


def _split_hi_lo(x):
    hi = x.astype(MXU_DTYPE)
    return hi, (x - hi.astype(jnp.float32)).astype(MXU_DTYPE)


def _adaln_kernel(c_ref, w_ref, b_ref, o_ref):
    c = c_ref[...]
    c_hi, c_lo = _split_hi_lo(c * jax.nn.sigmoid(c))
    w_hi, w_lo = _split_hi_lo(w_ref[0])
    o_ref[0] = _dot(c_hi, w_hi) + _dot(c_hi, w_lo) + _dot(c_lo, w_hi) + b_ref[0]


def _adaln(c, w_ada, b_ada):
    depth, d, n = w_ada.shape
    rows = 16
    c = jnp.pad(c, ((0, rows - c.shape[0]), (0, 0)))
    b = rows
    tn = 1024
    return pl.pallas_call(
        _adaln_kernel,
        grid=(depth, n // tn),
        in_specs=[
            pl.BlockSpec((b, d), lambda l, j: (0, 0)),
            pl.BlockSpec((1, d, tn), lambda l, j: (l, 0, j)),
            pl.BlockSpec((1, 1, tn), lambda l, j: (l, 0, j)),
        ],
        out_specs=pl.BlockSpec((1, b, tn), lambda l, j: (l, 0, j)),
        out_shape=jax.ShapeDtypeStruct((depth, b, n), jnp.float32),
        compiler_params=pltpu.CompilerParams(
            dimension_semantics=("arbitrary", "arbitrary"), vmem_limit_bytes=VMEM_LIMIT_BYTES),
        name="adaln",
    )(c, w_ada, b_ada.reshape(depth, 1, n))


def _ffn_kernel(x_ref, shift_ref, scale_ref, gate_ref, g_ref, wg_ref, wu_ref, wo_ref, gfin_ref, o_ref,
                h_scr, acc_scr, *, final_norm):
    j = pl.program_id(1)

    @pl.when(j == 0)
    def _():
        h = _modulated_norm(x_ref[...], g_ref[...], shift_ref[0], scale_ref[0])
        h_scr[...] = h.astype(h_scr.dtype)
        acc_scr[...] = jnp.zeros_like(acc_scr)

    h = h_scr[...]
    g = _dot(h, wg_ref[...])
    u = _dot(h, wu_ref[...])
    a = (g * jax.nn.sigmoid(g)) * u
    acc_scr[...] += _dot(a.astype(wo_ref.dtype), wo_ref[...])

    @pl.when(j == pl.num_programs(1) - 1)
    def _():
        y = x_ref[...] + (0.5 * gate_ref[0]) * acc_scr[...]
        if final_norm:
            y = _rms_scale(y) * gfin_ref[...]
        o_ref[...] = y


def _ffn(x2d, shift, scale, gate, g, w_i, w_o, g_final, *, seq, final_norm):
    n, d = x2d.shape
    tm, tf = 512, 512
    nf = D_FF // tf
    per_b = seq // tm
    mod_spec = pl.BlockSpec((1, 1, d), lambda i, j: (i // per_b, 0, 0))
    return pl.pallas_call(
        functools.partial(_ffn_kernel, final_norm=final_norm),
        grid=(n // tm, nf),
        in_specs=[
            pl.BlockSpec((tm, d), lambda i, j: (i, 0)),
            mod_spec, mod_spec, mod_spec,
            pl.BlockSpec((1, d), lambda i, j: (0, 0)),
            pl.BlockSpec((d, tf), lambda i, j: (0, j)),
            pl.BlockSpec((d, tf), lambda i, j: (0, j + nf)),
            pl.BlockSpec((tf, d), lambda i, j: (j, 0)),
            pl.BlockSpec((1, d), lambda i, j: (0, 0)),
        ],
        out_specs=pl.BlockSpec((tm, d), lambda i, j: (i, 0)),
        out_shape=jax.ShapeDtypeStruct((n, d), jnp.float32),
        scratch_shapes=[pltpu.VMEM((tm, d), MXU_DTYPE), pltpu.VMEM((tm, d), jnp.float32)],
        compiler_params=pltpu.CompilerParams(
            dimension_semantics=("arbitrary", "arbitrary"), vmem_limit_bytes=VMEM_LIMIT_BYTES),
        name="ffn",
    )(x2d, shift, scale, gate, g.reshape(1, d), w_i, w_i, w_o, g_final.reshape(1, d))


def _inproj_kernel(x_ref, shift_ref, scale_ref, g_ref, w_ref, o_ref, h_scr):
    @pl.when(pl.program_id(1) == 0)
    def _():
        h = _modulated_norm(x_ref[...], g_ref[...], shift_ref[0], scale_ref[0])
        h_scr[...] = h.astype(h_scr.dtype)

    o_ref[...] = _dot(h_scr[...], w_ref[...]).astype(o_ref.dtype)


def _inproj(x2d, shift, scale, g, w, *, seq):
    n, d = x2d.shape
    tm, tn = 512, N_IN_PAD // 4
    per_b = seq // tm
    mod_spec = pl.BlockSpec((1, 1, d), lambda i, j: (i // per_b, 0, 0))
    return pl.pallas_call(
        _inproj_kernel,
        grid=(n // tm, N_IN_PAD // tn),
        in_specs=[
            pl.BlockSpec((tm, d), lambda i, j: (i, 0)),
            mod_spec, mod_spec,
            pl.BlockSpec((1, d), lambda i, j: (0, 0)),
            pl.BlockSpec((d, tn), lambda i, j: (0, j)),
        ],
        out_specs=pl.BlockSpec((tm, tn), lambda i, j: (i, j)),
        out_shape=jax.ShapeDtypeStruct((n, N_IN_PAD), MXU_DTYPE),
        scratch_shapes=[pltpu.VMEM((tm, d), MXU_DTYPE)],
        compiler_params=pltpu.CompilerParams(
            dimension_semantics=("arbitrary", "arbitrary"), vmem_limit_bytes=VMEM_LIMIT_BYTES),
        name="inproj",
    )(x2d, shift, scale, g.reshape(1, d), w)


def _merge_kernel(x_ref, gate_ref, on_ref, od_ref, a0_ref, a1_ref, wun_ref, wud_ref, wout_ref, o_ref):
    yn = _dot(on_ref[...], wun_ref[...])
    yd = _dot(od_ref[...], wud_ref[...])
    a0 = jax.nn.sigmoid(a0_ref[...].astype(jnp.float32))
    a1 = jax.nn.sigmoid(a1_ref[...].astype(jnp.float32))
    y = a0 * yn + a1 * yd
    o_ref[...] = x_ref[...] + gate_ref[0] * _dot(y.astype(wout_ref.dtype), wout_ref[...])


def _merge(x2d, gate, o_n, o_d, proj, w_up_nsa, w_up_dsa, w_out, *, seq):
    n, d = x2d.shape
    tm = 512
    per_b = seq // tm
    hw = o_n.shape[1]
    const = dict(pipeline_mode=pl.Buffered(1))
    return pl.pallas_call(
        _merge_kernel,
        grid=(n // tm,),
        in_specs=[
            pl.BlockSpec((tm, d), lambda i: (i, 0)),
            pl.BlockSpec((1, 1, d), lambda i: (i // per_b, 0, 0)),
            pl.BlockSpec((tm, hw), lambda i: (i, 0)),
            pl.BlockSpec((tm, hw), lambda i: (i, 0)),
            pl.BlockSpec((tm, d), lambda i: (i, COL_AM // d)),
            pl.BlockSpec((tm, d), lambda i: (i, COL_AM // d + 1)),
            pl.BlockSpec((hw, d), lambda i: (0, 0), **const),
            pl.BlockSpec((hw, d), lambda i: (0, 0), **const),
            pl.BlockSpec((d, d), lambda i: (0, 0), **const),
        ],
        out_specs=pl.BlockSpec((tm, d), lambda i: (i, 0)),
        out_shape=jax.ShapeDtypeStruct((n, d), jnp.float32),
        compiler_params=pltpu.CompilerParams(
            dimension_semantics=("arbitrary",), vmem_limit_bytes=VMEM_LIMIT_BYTES),
        name="merge",
    )(x2d, gate, o_n, o_d, proj, proj, w_up_nsa, w_up_dsa, w_out)


def _dot_nt(a, b):
    return lax.dot_general(a, b, (((1,), (1,)), ((), ())), preferred_element_type=jnp.float32)


def _iota(shape, dim):
    return lax.broadcasted_iota(jnp.int32, shape, dim)


def _softmax_rows(s, mask):
    s = jnp.where(mask, s, NEG_INF)
    m = jnp.max(s, axis=1, keepdims=True)
    p = jnp.where(mask, jnp.exp(s - m), 0.0)
    return p * (1.0 / jnp.maximum(jnp.sum(p, axis=1, keepdims=True), 1e-30))


def _lane_tiles(x):
    return [x[:, u * 128:(u + 1) * 128] for u in range(x.shape[1] // 128)]


def _max_pass(s_parts, mask, rows0, mx_scr):
    qb = s_parts[0].shape[0]
    for r, s in enumerate(s_parts):
        tiles = _lane_tiles(jnp.where(mask, s, NEG_INF))
        red = functools.reduce(jnp.maximum, tiles)
        sl = slice(rows0 + r * qb, rows0 + (r + 1) * qb)
        mx_scr[sl, :] = jnp.maximum(mx_scr[sl, :], red)


def _exp_pass(s_parts, mask, shifts, v_tile, rows0, ls_scr, acc_scr):
    qb = s_parts[0].shape[0]
    ps = []
    for r, (s, shift) in enumerate(zip(s_parts, shifts)):
        if mask is not None:
            s = jnp.where(mask, s, NEG_INF)
        ptiles = [jnp.exp(t - shift) for t in _lane_tiles(s)]
        sl = slice(rows0 + r * qb, rows0 + (r + 1) * qb)
        ls_scr[sl, :] = ls_scr[sl, :] + functools.reduce(jnp.add, ptiles)
        ps.append(jnp.concatenate(ptiles, axis=1))
    p = jnp.concatenate(ps, axis=0).astype(v_tile.dtype)
    rows = slice(rows0, rows0 + len(s_parts) * qb)
    acc_scr[rows, :] = acc_scr[rows, :] + _dot(p, v_tile)


def _compress_kernel(x_ref, pe_ref, w1_ref, w2_ref, o_ref, *, transposed):
    x = x_ref[0, 0]
    half = x.shape[1]
    first = _dot(x, w1_ref[0:half, :])
    second = _dot(x, w1_ref[half:2 * half, :])
    pe_term = _dot(pe_ref[...], w1_ref[...])[0:1]
    hid = first + pltpu.roll(second, x.shape[0] - 1, 0) + pe_term
    hid = (hid * jax.nn.sigmoid(hid)).astype(w2_ref.dtype)
    if transposed:
        o_ref[0, 0] = _dot_nt(w2_ref[...], hid).astype(o_ref.dtype)
    else:
        o_ref[0, 0] = _dot(hid, w2_ref[...]).astype(o_ref.dtype)


def _compress(x_chunks, pe, w1, w2, *, transposed):
    b, g, nc, width = x_chunks.shape
    dk = NSA_HEAD_DIM
    pe8 = jnp.zeros((8, 2 * width), MXU_DTYPE).at[0].set(pe.reshape(-1).astype(MXU_DTYPE))
    w2m = (w2.T if transposed else w2).astype(MXU_DTYPE)
    oshape = (b, g, dk, nc) if transposed else (b, g, nc, dk)
    return pl.pallas_call(
        functools.partial(_compress_kernel, transposed=transposed),
        grid=(b, g),
        in_specs=[
            pl.BlockSpec((1, 1, nc, width), lambda i, j: (i, j, 0, 0)),
            pl.BlockSpec((8, 2 * width), lambda i, j: (0, 0)),
            pl.BlockSpec((2 * width, NSA_CMP_HIDDEN), lambda i, j: (0, 0)),
            pl.BlockSpec(w2m.shape, lambda i, j: (0, 0)),
        ],
        out_specs=pl.BlockSpec((1, 1) + oshape[2:], lambda i, j: (i, j, 0, 0)),
        out_shape=jax.ShapeDtypeStruct(oshape, MXU_DTYPE),
        compiler_params=pltpu.CompilerParams(dimension_semantics=("arbitrary", "arbitrary")),
        name="nsa_compress",
    )(x_chunks, pe8, w1.astype(MXU_DTYPE), w2m)


def _ckvnorm_kernel(x_ref, g_ref, o_ref):
    x = x_ref[...].astype(jnp.float32)
    o_ref[...] = (_rms_scale(x) * g_ref[...]).astype(o_ref.dtype)


def _ckvnorm(proj, g_kv):
    n = proj.shape[0]
    tm = 2048
    return pl.pallas_call(
        _ckvnorm_kernel,
        grid=(n // tm,),
        in_specs=[pl.BlockSpec((tm, DSA_KV_LATENT), lambda i: (i, COL_CKV // DSA_KV_LATENT)),
                  pl.BlockSpec((1, DSA_KV_LATENT), lambda i: (0, 0))],
        out_specs=pl.BlockSpec((tm, DSA_KV_LATENT), lambda i: (i, 0)),
        out_shape=jax.ShapeDtypeStruct((n, DSA_KV_LATENT), MXU_DTYPE),
        compiler_params=pltpu.CompilerParams(dimension_semantics=("arbitrary",)),
        name="ckv_norm",
    )(proj, g_kv.reshape(1, -1).astype(jnp.float32))


def _nsa_kernel(far_ref, q_ref, small_ref, kct_ref, vc_ref, kst_ref, vs_ref, kwt_ref, vw_ref, bw_ref, bc_ref,
                ovt_ref, o_ref, qs_scr, sel_scr, mxf_scr, mb_scr, ls_scr, acc_scr, sn_scr, oc_scr, ow_scr):
    i = pl.program_id(1)
    t0 = i * Q_BLOCK
    qb, dk, hg = Q_BLOCK, NSA_HEAD_DIM, NSA_HEADS // NSA_KV_GROUPS
    grows = hg * qb
    nc = kct_ref.shape[2]
    nb = ovt_ref.shape[0]
    wk = qb + NSA_WINDOW
    f32 = jnp.float32

    gates = jax.nn.sigmoid(small_ref[0].astype(f32))
    n_io = _iota((qb, nc), 1)
    tc_io = t0 + _iota((qb, nc), 0)
    mask_c = NSA_CMP_STRIDE * n_io + (NSA_CMP_LEN - 1) <= tc_io
    near_lo = (qb // NSA_CMP_STRIDE) * i - 16
    near_c = n_io >= near_lo
    shift_c = jnp.where(near_lo >= 0, near_lo, near_lo + nc)
    j_io = _iota((nb, qb), 0)
    tj_io = t0 + _iota((nb, qb), 1)
    cur = jnp.right_shift(tj_io, 6)
    forced = (j_io == 0) | (j_io == cur) | (j_io == cur - 1)
    visible = j_io * NSA_SEL_LEN <= tj_io
    eye = jnp.where(_iota((qb, qb), 0) == _iota((qb, qb), 1), 1.0, 0.0).astype(MXU_DTYPE)
    kw_io = _iota((qb, wk), 1)
    off_w = _iota((qb, wk), 0) - kw_io + NSA_WINDOW
    mask_w = (off_w >= 0) & (off_w < NSA_WINDOW) & (t0 - NSA_WINDOW + kw_io >= 0)
    kn_io = _iota((qb, 2 * qb), 1)
    pos_n = t0 - qb + kn_io
    causal_n = (pos_n <= t0 + _iota((qb, 2 * qb), 0)) & (pos_n >= 0)
    n_far = (i + 2) // 4
    far_blocks = 2 * (i - 1)
    jsel_io = _iota((qb, nb), 1)
    qs_scr[...] = q_ref[0, 0] * (dk ** -0.5)

    for g in range(NSA_KV_GROUPS):
        rows = pl.ds(g * grows, grows)
        gsl = slice(g * dk, (g + 1) * dk)
        qg = qs_scr[g * grows:(g + 1) * grows, :]

        s = _dot(qg, kct_ref[0, gsl, :])
        probs = []
        for r in range(hg):
            h = g * hg + r
            bias = jnp.where(near_c, pltpu.roll(bc_ref[h], shift_c, 1), far_ref[h])
            probs.append(_softmax_rows(s[r * qb:(r + 1) * qb] + bias, mask_c))
        pstack = jnp.concatenate(probs, axis=0)
        oc_scr[rows, :] = _dot(pstack.astype(vc_ref.dtype), vc_ref[0])
        pg = probs[0] + probs[1] + probs[2] + probs[3]
        hi = pg.astype(MXU_DTYPE)
        r1 = pg - hi.astype(f32)
        mid = r1.astype(MXU_DTYPE)
        lo = (r1 - mid.astype(f32)).astype(MXU_DTYPE)
        ovt = ovt_ref[...]
        imp = _dot_nt(ovt, hi) + _dot_nt(ovt, mid) + _dot_nt(ovt, lo)
        imp = jnp.where(visible, imp + jnp.where(forced, FORCE_BONUS, 0.0), NEG_INF)
        ranks = [jnp.zeros((nb, qb), f32) for _ in range(4)]
        for jp in range(nb):
            row = imp[jp:jp + 1, :]
            beats = jnp.where(row > imp, 1.0, jnp.where(row == imp, jnp.where(jp < j_io, 1.0, 0.0), 0.0))
            ranks[jp % 4] = ranks[jp % 4] + beats
        rank = (ranks[0] + ranks[1]) + (ranks[2] + ranks[3])
        sel_t = jnp.where(rank < NSA_SEL_BLOCKS, 1.0, 0.0).astype(MXU_DTYPE)
        sel = _dot_nt(eye, sel_t)
        sel_scr[2 * g] = sel
        sel_scr[2 * g + 1] = jnp.where(jsel_io < far_blocks, sel, 0.0)

        kwt = jnp.concatenate([kwt_ref[0, i + u, gsl, :] for u in range(wk // qb)], axis=1)
        s = _dot(qg, kwt)
        probs, sums = [], []
        for r in range(hg):
            sh = jnp.where(mask_w, s[r * qb:(r + 1) * qb] + bw_ref[g * hg + r], NEG_INF)
            p = jnp.exp(sh - jnp.max(sh, axis=1, keepdims=True))
            probs.append(p)
            sums.append(jnp.sum(p, axis=1, keepdims=True))
        pstack = jnp.concatenate(probs, axis=0)
        ow = _dot(pstack.astype(vw_ref.dtype), vw_ref[0, pl.ds(pl.multiple_of(t0, qb), wk), :])
        for r in range(hg):
            ow_scr[g * grows + r * qb:g * grows + (r + 1) * qb, :] = ow[r * qb:(r + 1) * qb] * (1.0 / sums[r])

    def far_tile(jt, g):
        gsl = slice(g * dk, (g + 1) * dk)
        kt = jnp.concatenate([kst_ref[0, 1 + 4 * jt + u, gsl, :] for u in range(4)], axis=1)
        expand = jnp.where(_iota((nb, 512), 0) == 8 * jt + jnp.right_shift(_iota((nb, 512), 1), 6),
                           1.0, 0.0).astype(MXU_DTYPE)
        mask = _dot(sel_scr[2 * g + 1].astype(MXU_DTYPE), expand) > 0.5
        sc = _dot(qs_scr[g * grows:(g + 1) * grows, :], kt)
        return [sc[r * qb:(r + 1) * qb] for r in range(hg)], mask

    mxf_scr[...] = jnp.full(mxf_scr.shape, NEG_INF, f32)
    ls_scr[...] = jnp.zeros(ls_scr.shape, f32)
    acc_scr[...] = jnp.zeros(acc_scr.shape, f32)

    def far_max_body(jt):
        for g in range(NSA_KV_GROUPS):
            parts, mask = far_tile(jt, g)
            _max_pass(parts, mask, g * grows, mxf_scr)

    _for_each_tile_paired(n_far, far_max_body)

    expand_n = jnp.where(_iota((nb, 2 * qb), 0) == 2 * (i - 1) + jnp.right_shift(_iota((nb, 2 * qb), 1), 6),
                         1.0, 0.0).astype(MXU_DTYPE)
    for g in range(NSA_KV_GROUPS):
        gsl = slice(g * dk, (g + 1) * dk)
        kt = jnp.concatenate([kst_ref[0, i + u, gsl, :] for u in range(2)], axis=1)
        mask = (_dot(sel_scr[2 * g].astype(MXU_DTYPE), expand_n) > 0.5) & causal_n
        sc = _dot(qs_scr[g * grows:(g + 1) * grows, :], kt)
        for r in range(hg):
            h = g * hg + r
            hr = slice(g * grows + r * qb, g * grows + (r + 1) * qb)
            sn = jnp.where(mask, sc[r * qb:(r + 1) * qb] + bw_ref[h][:, NSA_WINDOW - qb:], NEG_INF)
            sn_scr[hr, :] = sn
            m_far = jnp.max(mxf_scr[hr, :], axis=1, keepdims=True) + far_ref[h]
            m_row = jnp.maximum(m_far, jnp.max(sn, axis=1, keepdims=True))
            mb_scr[hr, :] = jnp.broadcast_to(m_row, (qb, 128))

    def far_exp_body(jt):
        vt = vs_ref[0, pl.ds(pl.multiple_of(qb + 512 * jt, qb), 512), :]
        for g in range(NSA_KV_GROUPS):
            parts, mask = far_tile(jt, g)
            shifts = [mb_scr[g * grows + r * qb:g * grows + (r + 1) * qb, :] - far_ref[g * hg + r]
                      for r in range(hg)]
            _exp_pass(parts, mask, shifts, vt, g * grows, ls_scr, acc_scr)

    _for_each_tile_paired(n_far, far_exp_body)

    vt_n = vs_ref[0, pl.ds(pl.multiple_of(t0, qb), 2 * qb), :]
    for g in range(NSA_KV_GROUPS):
        hrs = [slice(g * grows + r * qb, g * grows + (r + 1) * qb) for r in range(hg)]
        _exp_pass([sn_scr[hr, :] for hr in hrs], None, [mb_scr[hr, :] for hr in hrs], vt_n, g * grows,
                  ls_scr, acc_scr)

    for g in range(NSA_KV_GROUPS):
        slab = slice((g // 2) * 128, (g // 2) * 128 + 128)
        placed = []
        for r in range(hg):
            h = g * hg + r
            hr = slice(g * grows + r * qb, g * grows + (r + 1) * qb)
            o_s = acc_scr[hr, slab] * (1.0 / jnp.sum(ls_scr[hr, :], axis=1, keepdims=True))
            o_h = (gates[:, 3 * h:3 * h + 1] * oc_scr[hr, slab] + gates[:, 3 * h + 1:3 * h + 2] * o_s
                   + gates[:, 3 * h + 2:3 * h + 3] * ow_scr[hr, slab])
            if g % 2 != r % 2:
                o_h = pltpu.roll(o_h, dk, 1)
            placed.append(o_h)
        low_half = _iota((qb, 128), 1) < dk
        for pr in range(hg // 2):
            pair = jnp.where(low_half, placed[2 * pr], placed[2 * pr + 1])
            col = (g * hg // 2 + pr) * 128
            o_ref[0, :, col:col + 128] = pair.astype(o_ref.dtype)


def _nsa(proj3, q_r, kct, vc, kst, vs, kwt, vw, bias_w, bias_c, ovt, far):
    b, s, _ = proj3.shape
    nq = s // Q_BLOCK
    rows = NSA_HEADS * Q_BLOCK
    vdim = NSA_KV_WIDTH
    const = dict(pipeline_mode=pl.Buffered(1))

    def whole(a):
        nd = a.ndim
        return pl.BlockSpec((1,) + a.shape[1:], lambda bi, i: (bi,) + (0,) * (nd - 1))

    return pl.pallas_call(
        _nsa_kernel,
        grid=(b, nq),
        in_specs=[
            pl.BlockSpec(memory_space=pltpu.SMEM),
            pl.BlockSpec((1, 1, rows, NSA_HEAD_DIM), lambda bi, i: (bi, i, 0, 0)),
            pl.BlockSpec((1, Q_BLOCK, 128), lambda bi, i: (bi, i, COL_SMALL // 128)),
            whole(kct), whole(vc), whole(kst), whole(vs), whole(kwt), whole(vw),
            pl.BlockSpec(bias_w.shape, lambda bi, i: (0, 0, 0), **const),
            pl.BlockSpec(bias_c.shape, lambda bi, i: (0, 0, 0), **const),
            pl.BlockSpec(ovt.shape, lambda bi, i: (0, 0), **const),
        ],
        out_specs=pl.BlockSpec((1, Q_BLOCK, NSA_HEADS * NSA_HEAD_DIM), lambda bi, i: (bi, i, 0)),
        out_shape=jax.ShapeDtypeStruct((b, s, NSA_HEADS * NSA_HEAD_DIM), MXU_DTYPE),
        scratch_shapes=[pltpu.VMEM((rows, NSA_HEAD_DIM), MXU_DTYPE),
                        pltpu.VMEM((2 * NSA_KV_GROUPS, Q_BLOCK, ovt.shape[0]), jnp.float32),
                        pltpu.VMEM((rows, 128), jnp.float32),
                        pltpu.VMEM((rows, 128), jnp.float32),
                        pltpu.VMEM((rows, 128), jnp.float32),
                        pltpu.VMEM((rows, vdim), jnp.float32),
                        pltpu.VMEM((rows, 2 * Q_BLOCK), jnp.float32),
                        pltpu.VMEM((rows, vdim), jnp.float32),
                        pltpu.VMEM((rows, vdim), jnp.float32)],
        compiler_params=pltpu.CompilerParams(
            dimension_semantics=("arbitrary", "arbitrary"), vmem_limit_bytes=VMEM_LIMIT_BYTES),
        name="nsa_attention",
    )(far, q_r, proj3, kct, vc, kst, vs, kwt, vw, bias_w, bias_c, ovt)


INT_MIN = -2 ** 31
DSA_ATT_HEADS_PER_CHUNK = 4


def _for_each_tile_paired(n, body):
    def pair(p, carry):
        body(2 * p)
        body(2 * p + 1)
        return carry

    lax.fori_loop(0, n // 2, pair, 0)

    @pl.when(n % 2 == 1)
    def _():
        body(n - 1)


def _sortable_key(x):
    bits = pltpu.bitcast(x, jnp.int32)
    return bits ^ (jnp.right_shift(bits, 31) & 0x7FFFFFFF)


def _dsa_kernel(far_ref, qd_ref, qit_ref, wt_ref, ki_ref, ct_ref, c_ref, wuk_ref, wuv_ref, bd_ref, o_ref,
                qlat_scr, keyt_scr, key_scr, mxf_scr, mb_scr, ls_scr, acc_scr, sn_scr, tie_scr):
    i = pl.program_id(1)
    t0 = i * Q_BLOCK
    qb = Q_BLOCK
    f32 = jnp.float32
    k_top = float(DSA_TOPK_MAX)
    chunk = 4 * qb
    n_chunks = DSA_HEADS // 4

    for pr in range(DSA_HEADS // 2):
        r = _dot(qd_ref[0, :, pr * 128:(pr + 1) * 128], wuk_ref[pr]) * (DSA_HEAD_DIM ** -0.5)
        qlat_scr[2 * pr * qb:(2 * pr + 1) * qb, :] = r[:, :DSA_KV_LATENT].astype(qlat_scr.dtype)
        qlat_scr[(2 * pr + 1) * qb:(2 * pr + 2) * qb, :] = r[:, DSA_KV_LATENT:].astype(qlat_scr.dtype)

    n_tiles = i // 4 + 1
    w_t = wt_ref[0, 0].astype(f32) * INDEX_SCALE
    tq_io = t0 + _iota((512, qb), 1)
    keyt_scr[0:qb, :] = jnp.full((qb, qb), INT_MIN, jnp.int32)

    def tile_rows(jt):
        return pl.ds(pl.multiple_of(qb + 512 * jt, qb), 512)

    def index_body(jt):
        kt = ki_ref[0, pl.ds(pl.multiple_of(512 * jt, 512), 512), :]
        sc = jnp.zeros((512, qb), f32)
        for hc in range(n_chunks):
            lg = _dot(kt, qit_ref[0, 0, :, hc * chunk:(hc + 1) * chunk])
            for r in range(4):
                h = hc * 4 + r
                sc = sc + jnp.maximum(lg[:, r * qb:(r + 1) * qb], 0.0) * w_t[h:h + 1, :]
        sc = jnp.where(512 * jt + _iota((512, qb), 0) <= tq_io, sc, NEG_INF)
        keyt_scr[tile_rows(jt), :] = _sortable_key(sc)

    _for_each_tile_paired(n_tiles, index_body)

    def count(pred):
        def body(jt, c):
            x = jnp.where(pred(keyt_scr[tile_rows(jt), :], 512 * jt + _iota((512, qb), 0)), 1.0, 0.0)
            parts = [x[8 * v:8 * v + 8] for v in range(512 // 8)]
            while len(parts) > 1:
                parts = [a + b for a, b in zip(parts[0::2], parts[1::2])]
            return c + parts[0]
        c = lax.fori_loop(0, n_tiles, body, jnp.zeros((8, qb), f32))
        return jnp.sum(c, axis=0, keepdims=True)

    thr = jnp.where(count(lambda kk, pos: kk >= 0) >= k_top, 0, INT_MIN).astype(jnp.int32)
    thr = thr + jnp.zeros((1, qb), jnp.int32)

    def bit_body(bit, thr):
        cand = thr | jnp.left_shift(jnp.int32(1), 30 - bit)
        return jnp.where(count(lambda kk, pos: kk >= cand) >= k_top, cand, thr)

    thr = lax.fori_loop(0, 31, bit_body, thr)
    need = k_top - count(lambda kk, pos: kk > thr)
    n_ge = count(lambda kk, pos: kk >= thr)
    tie_scr[...] = jnp.full((1, qb), 2 ** 30, jnp.int32)
    has_excess = jnp.max(jnp.where(n_ge > k_top, 1.0, 0.0)) > 0.5

    @pl.when(has_excess)
    def _():
        def pos_body(bit, last):
            cand = last | jnp.left_shift(jnp.int32(1), 12 - bit)
            n_before = count(lambda kk, pos: (kk == thr) & (pos < cand))
            return jnp.where(n_before < need, cand, last)
        tie_scr[...] = lax.fori_loop(0, 13, pos_body, jnp.zeros((1, qb), jnp.int32))

    last_tie = tie_scr[...]
    eye = jnp.where(_iota((qb, qb), 0) == _iota((qb, qb), 1), 1.0, 0.0).astype(MXU_DTYPE)

    def selected_rows(kk, pos, extra=None):
        chosen = (kk > thr) | ((kk == thr) & (pos <= last_tie))
        if extra is not None:
            chosen = chosen & extra
        return _dot_nt(eye, jnp.where(chosen, 1.0, 0.0).astype(MXU_DTYPE)) > 0.5

    pos_n = t0 - qb + _iota((2 * qb, qb), 0)
    causal_n = (pos_n <= t0 + _iota((2 * qb, qb), 1)) & (pos_n >= 0)
    mask_n = selected_rows(keyt_scr[pl.ds(pl.multiple_of(t0, qb), 2 * qb), :], pos_n, causal_n)

    def mask_body(jt, carry):
        mt = selected_rows(keyt_scr[tile_rows(jt), :], 512 * jt + _iota((512, qb), 0))
        for u in range(4):
            bi = 4 * jt + u
            key_scr[1 + bi] = jnp.where(mt[:, u * qb:(u + 1) * qb], jnp.where(bi < i - 1, 1, 0), 0)
        return carry

    lax.fori_loop(0, n_tiles, mask_body, 0)

    mxf_scr[...] = jnp.full(mxf_scr.shape, NEG_INF, f32)
    ls_scr[...] = jnp.zeros(ls_scr.shape, f32)
    acc_scr[...] = jnp.zeros(acc_scr.shape, f32)
    n_far = (i + 2) // 4

    hpc = DSA_ATT_HEADS_PER_CHUNK
    arows = hpc * qb
    n_att = DSA_HEADS // hpc

    def far_tile(jt, hc):
        ct = jnp.concatenate([ct_ref[0, 1 + 4 * jt + u] for u in range(4)], axis=1)
        sc = _dot(qlat_scr[hc * arows:(hc + 1) * arows, :], ct)
        return [sc[r * qb:(r + 1) * qb] for r in range(hpc)]

    def far_mask(jt):
        return jnp.concatenate([key_scr[1 + 4 * jt + u] for u in range(4)], axis=1) > 0

    def far_max_body(jt):
        mask = far_mask(jt)
        for hc in range(n_att):
            _max_pass(far_tile(jt, hc), mask, hc * arows, mxf_scr)

    _for_each_tile_paired(n_far, far_max_body)

    ct_n = jnp.concatenate([ct_ref[0, i], ct_ref[0, i + 1]], axis=1)
    for hc in range(n_att):
        sc = _dot(qlat_scr[hc * arows:(hc + 1) * arows, :], ct_n)
        for r in range(hpc):
            h = hc * hpc + r
            hr = slice(h * qb, (h + 1) * qb)
            sn = jnp.where(mask_n, sc[r * qb:(r + 1) * qb] + bd_ref[h], NEG_INF)
            sn_scr[hr, :] = sn
            m_far = jnp.max(mxf_scr[hr, :], axis=1, keepdims=True) + far_ref[NSA_HEADS + h]
            m_row = jnp.maximum(m_far, jnp.max(sn, axis=1, keepdims=True))
            mb_scr[hr, :] = jnp.broadcast_to(m_row, (qb, 128))

    def far_exp_body(jt):
        mask = far_mask(jt)
        cv = c_ref[0, pl.ds(pl.multiple_of(qb + 512 * jt, qb), 512), :]
        for hc in range(n_att):
            shifts = [mb_scr[(hc * hpc + r) * qb:(hc * hpc + r + 1) * qb, :] - far_ref[NSA_HEADS + hc * hpc + r]
                      for r in range(hpc)]
            _exp_pass(far_tile(jt, hc), mask, shifts, cv, hc * arows, ls_scr, acc_scr)

    _for_each_tile_paired(n_far, far_exp_body)

    cv_n = c_ref[0, pl.ds(pl.multiple_of(t0, qb), 2 * qb), :]
    for hc in range(n_att):
        hrs = [slice((hc * hpc + r) * qb, (hc * hpc + r + 1) * qb) for r in range(hpc)]
        _exp_pass([sn_scr[hr, :] for hr in hrs], None, [mb_scr[hr, :] for hr in hrs], cv_n, hc * arows,
                  ls_scr, acc_scr)

    def normalised(hr):
        return acc_scr[hr, :] * (1.0 / jnp.sum(ls_scr[hr, :], axis=1, keepdims=True))

    for pr in range(DSA_HEADS // 2):
        ra = slice(2 * pr * qb, (2 * pr + 1) * qb)
        rb = slice((2 * pr + 1) * qb, (2 * pr + 2) * qb)
        o_lat = jnp.concatenate([normalised(ra), normalised(rb)], axis=1)
        o_ref[0, :, pr * 128:(pr + 1) * 128] = _dot(o_lat.astype(wuv_ref.dtype), wuv_ref[pr]).astype(o_ref.dtype)


def _dsa(proj3, qi_t, w_t, ki, ct, c_pad, wuk_bd, wuv_bd, bias_d, far):
    b, s, _ = proj3.shape
    nq = s // Q_BLOCK
    rows = DSA_HEADS * Q_BLOCK
    n_tiles = -(-nq // 4)
    const = dict(pipeline_mode=pl.Buffered(1))

    def whole(a):
        nd = a.ndim
        return pl.BlockSpec((1,) + a.shape[1:], lambda bi, i: (bi,) + (0,) * (nd - 1))

    return pl.pallas_call(
        _dsa_kernel,
        grid=(b, nq),
        in_specs=[
            pl.BlockSpec(memory_space=pltpu.SMEM),
            pl.BlockSpec((1, Q_BLOCK, DSA_HEADS * DSA_HEAD_DIM), lambda bi, i: (bi, i, COL_QD // 1024)),
            pl.BlockSpec((1, 1, IDX_DIM, rows), lambda bi, i: (bi, i, 0, 0)),
            pl.BlockSpec((1, 1, IDX_HEADS, Q_BLOCK), lambda bi, i: (bi, i, 0, 0)),
            whole(ki), whole(ct), whole(c_pad),
            pl.BlockSpec(wuk_bd.shape, lambda bi, i: (0, 0, 0), **const),
            pl.BlockSpec(wuv_bd.shape, lambda bi, i: (0, 0, 0), **const),
            pl.BlockSpec(bias_d.shape, lambda bi, i: (0, 0, 0), **const),
        ],
        out_specs=pl.BlockSpec((1, Q_BLOCK, DSA_HEADS * DSA_HEAD_DIM), lambda bi, i: (bi, i, 0)),
        out_shape=jax.ShapeDtypeStruct((b, s, DSA_HEADS * DSA_HEAD_DIM), MXU_DTYPE),
        scratch_shapes=[pltpu.VMEM((rows, DSA_KV_LATENT), MXU_DTYPE),
                        pltpu.VMEM((Q_BLOCK + 512 * n_tiles, Q_BLOCK), jnp.int32),
                        pltpu.VMEM((1 + 4 * n_tiles, Q_BLOCK, Q_BLOCK), jnp.int32),
                        pltpu.VMEM((rows, 128), jnp.float32),
                        pltpu.VMEM((rows, 128), jnp.float32),
                        pltpu.VMEM((rows, 128), jnp.float32),
                        pltpu.VMEM((rows, DSA_KV_LATENT), jnp.float32),
                        pltpu.VMEM((rows, 2 * Q_BLOCK), jnp.float32),
                        pltpu.VMEM((1, Q_BLOCK), jnp.int32)],
        compiler_params=pltpu.CompilerParams(
            dimension_semantics=("arbitrary", "arbitrary"), vmem_limit_bytes=VMEM_LIMIT_BYTES),
        name="dsa_attention",
    )(far, proj3, qi_t, w_t, ki, ct, c_pad, wuk_bd, wuv_bd, bias_d)


def _t5_bucket(dist):
    n = jnp.maximum(dist, 0)
    exact = REL_BUCKETS // 2
    nf = jnp.maximum(n, 1).astype(jnp.float32)
    large = exact + (jnp.log(nf / exact) / math.log(REL_MAX_DIST / exact)
                     * (REL_BUCKETS - exact)).astype(jnp.int32)
    large = jnp.minimum(large, REL_BUCKETS - 1)
    return jnp.where(n < exact, n, large)


def _bias_tables(rel_bias, nc):
    n_pos, n_neg = 1024, NSA_CMP_STRIDE * nc + 1024
    per_dist = rel_bias[_t5_bucket(jnp.arange(n_pos))]
    ext = jnp.concatenate([jnp.broadcast_to(per_dist[:1], (n_neg, per_dist.shape[1])), per_dist], axis=0)
    flipped = ext[::-1].T

    def skew(vec, rows):
        m = vec.shape[-1]
        tiled = jnp.tile(vec, (1,) * (vec.ndim - 1) + (rows,))
        return tiled[..., :rows * (m - 1)].reshape(vec.shape[:-1] + (rows, m - 1))

    def toeplitz(offset, count, heads):
        start = n_pos - 1 - offset - (Q_BLOCK - 1)
        seg = flipped[heads, start:start + count + Q_BLOCK]
        return skew(seg, Q_BLOCK)[..., Q_BLOCK - 1:Q_BLOCK - 1 + count]

    nsa, dsa = slice(0, NSA_HEADS), slice(NSA_HEADS, NSA_HEADS + DSA_HEADS)
    bias_w = toeplitz(NSA_WINDOW, Q_BLOCK + NSA_WINDOW, nsa)
    bias_d = toeplitz(Q_BLOCK, 2 * Q_BLOCK, dsa)
    st, qa = NSA_CMP_STRIDE, Q_BLOCK // NSA_CMP_STRIDE
    base = n_pos - 1 - 225 - (st - 1) - st * (qa - 1)
    seg = flipped[nsa, base:base + st * (nc + qa)].reshape(NSA_HEADS, nc + qa, st)
    per_b = jnp.transpose(seg[..., ::-1], (0, 2, 1))
    bias_c = skew(per_b, qa)[..., qa - 1:qa - 1 + nc]
    bias_c = jnp.transpose(bias_c, (0, 2, 1, 3)).reshape(NSA_HEADS, Q_BLOCK, nc)
    return bias_w, bias_c, bias_d, rel_bias[REL_BUCKETS - 1]


def _overlap_t(s):
    n_blk, nc = s // NSA_SEL_LEN, s // NSA_CMP_STRIDE
    cmp_start = np.arange(nc) * NSA_CMP_STRIDE
    sel_start = np.arange(n_blk) * NSA_SEL_LEN
    ov = (cmp_start[None, :] < sel_start[:, None] + NSA_SEL_LEN) & (cmp_start[None, :] + NSA_CMP_LEN > sel_start[:, None])
    return jnp.asarray(ov.astype(np.float32)).astype(MXU_DTYPE)


def _block_diag_pairs(w_uk, w_uv):
    hp = DSA_HEADS // 2
    dh, lat = DSA_HEAD_DIM, DSA_KV_LATENT
    uk = jnp.transpose(w_uk, (1, 2, 0)).reshape(hp, 2, dh, lat)
    uv = jnp.transpose(w_uv, (1, 0, 2)).reshape(hp, 2, lat, dh)
    z_k, z_v = jnp.zeros((hp, dh, lat), w_uk.dtype), jnp.zeros((hp, lat, dh), w_uv.dtype)
    wuk_bd = jnp.concatenate([jnp.concatenate([uk[:, 0], z_k], axis=2),
                              jnp.concatenate([z_k, uk[:, 1]], axis=2)], axis=1)
    wuv_bd = jnp.concatenate([jnp.concatenate([uv[:, 0], z_v], axis=2),
                              jnp.concatenate([z_v, uv[:, 1]], axis=2)], axis=1)
    return wuk_bd.astype(MXU_DTYPE), wuv_bd.astype(MXU_DTYPE)


def _mixers(proj, b, s, pe_k, pe_v, wk1, wk2, wv1, wv2, g_kv, w_uk, w_uv, tables, ovt):
    bias_w, bias_c, bias_d, far = tables
    p3 = proj.reshape(b, s, N_IN_PAD)
    nq, nblk = s // Q_BLOCK, s // Q_BLOCK
    g, dk = NSA_KV_GROUPS, NSA_HEAD_DIM

    def piece(off, width):
        return p3[:, :, off:off + width]

    def heads_major(a, heads, dim):
        return a.reshape(b, nq, Q_BLOCK, heads, dim).transpose(0, 1, 3, 2, 4).reshape(b, nq, heads * Q_BLOCK, dim)

    def chunked(a):
        return a.reshape(b, s // 16, 16, g, dk).transpose(0, 3, 1, 2, 4).reshape(b, g, s // 16, 16 * dk)

    def keys_t(a, pad_blocks):
        t = a.reshape(b, nblk, Q_BLOCK, a.shape[-1]).transpose(0, 1, 3, 2)
        return jnp.pad(t, ((0, 0), (pad_blocks, 0), (0, 0), (0, 0)))

    def rows_padded(a, pad):
        return jnp.pad(a, ((0, 0), (pad, 0), (0, 0)))

    kct = _compress(chunked(piece(COL_KC, 256)), pe_k, wk1, wk2, transposed=True).reshape(b, g * dk, s // 16)
    vcm = _compress(chunked(piece(COL_VC, 256)), pe_v, wv1, wv2, transposed=False)
    vcm = vcm.transpose(0, 2, 1, 3).reshape(b, s // 16, g * dk)
    o_n = _nsa(p3, heads_major(piece(COL_QN, 1024), NSA_HEADS, dk), kct, vcm,
               keys_t(piece(COL_KS, 256), 1), rows_padded(piece(COL_VS, 256), Q_BLOCK),
               keys_t(piece(COL_KW, 256), NSA_WINDOW // Q_BLOCK), rows_padded(piece(COL_VW, 256), NSA_WINDOW),
               bias_w, bias_c, ovt, far)

    ckv = _ckvnorm(proj, g_kv).reshape(b, s, DSA_KV_LATENT)
    ki = piece(COL_SMALL + 48, IDX_DIM)
    qi_t = piece(COL_QI, 512).reshape(b, nq, Q_BLOCK, IDX_HEADS, IDX_DIM).transpose(0, 1, 4, 3, 2)
    qi_t = qi_t.reshape(b, nq, IDX_DIM, IDX_HEADS * Q_BLOCK)
    w_t = piece(COL_SMALL + 80, IDX_HEADS).reshape(b, nq, Q_BLOCK, IDX_HEADS).transpose(0, 1, 3, 2)
    wuk_bd, wuv_bd = _block_diag_pairs(w_uk, w_uv)
    o_d = _dsa(p3, qi_t, w_t, ki, keys_t(ckv, 1), rows_padded(ckv, Q_BLOCK), wuk_bd, wuv_bd, bias_d, far)
    return o_n.reshape(b * s, -1), o_d.reshape(b * s, -1)


def _permute_w_in(w_in_l):
    d = w_in_l.shape[0]
    sizes = (1024, 256, 256, 256, 256, 256, 256, 48, 1024, 256, 512, 32, 16, 4096)
    offs = np.concatenate([[0], np.cumsum(sizes)])
    (q_n, kc, vc, ks, vs, kw, vw, g_n, q_d, ckv, qi, ki, wi, a_m) = [
        w_in_l[:, int(offs[k]):int(offs[k + 1])] for k in range(len(sizes))]
    small = jnp.concatenate([g_n, ki, wi, jnp.zeros((d, 32), w_in_l.dtype)], axis=1)
    out = jnp.concatenate([a_m, q_n, q_d, kc, vc, ks, vs, kw, vw, qi, ckv, small,
                           jnp.zeros((d, N_IN_PAD - 8576), w_in_l.dtype)], axis=1)
    return out.astype(MXU_DTYPE)


def kernel(x, c, w_ada, b_ada, g_norm, w_ffn_in, w_ffn_out, w_in, nsa_pe_k, nsa_pe_v, nsa_cmp_k1, nsa_cmp_k2,
           nsa_cmp_v1, nsa_cmp_v2, dsa_g_kv, dsa_w_uk, dsa_w_uv, w_up_nsa, w_up_dsa, w_out, rel_bias, g_final):
    B, S, D = x.shape
    N = B * S
    mod = _adaln(c, w_ada, b_ada)[:, :B].reshape(DEPTH, B, N_SUBLAYERS, 3, 1, D)
    tables = _bias_tables(rel_bias, S // NSA_CMP_STRIDE)
    ovt = _overlap_t(S)
    x2d = x.reshape(N, D)
    for l in range(DEPTH):
        m0, m1, m2 = mod[l, :, 0], mod[l, :, 1], mod[l, :, 2]
        x2d = _ffn(x2d, m0[:, 0], m0[:, 1], m0[:, 2], g_norm[l, 0],
                   w_ffn_in[l, 0].astype(MXU_DTYPE), w_ffn_out[l, 0].astype(MXU_DTYPE), g_final,
                   seq=S, final_norm=False)
        proj = _inproj(x2d, m1[:, 0], m1[:, 1], g_norm[l, 1], _permute_w_in(w_in[l]), seq=S)
        o_n, o_d = _mixers(proj, B, S, nsa_pe_k[l], nsa_pe_v[l], nsa_cmp_k1[l], nsa_cmp_k2[l], nsa_cmp_v1[l],
                           nsa_cmp_v2[l], dsa_g_kv[l], dsa_w_uk[l], dsa_w_uv[l], tables, ovt)
        x2d = _merge(x2d, m1[:, 2], o_n, o_d, proj, w_up_nsa[l].astype(MXU_DTYPE),
                     w_up_dsa[l].astype(MXU_DTYPE), w_out[l].astype(MXU_DTYPE), seq=S)
        x2d = _ffn(x2d, m2[:, 0], m2[:, 1], m2[:, 2], g_norm[l, 2],
                   w_ffn_in[l, 1].astype(MXU_DTYPE), w_ffn_out[l, 1].astype(MXU_DTYPE), g_final,
                   seq=S, final_norm=(l == DEPTH - 1))
    return x2d.reshape(B, S, D)
```

```python
import functools
import math

import jax
import jax.numpy as jnp
import numpy as np
from jax import lax
from jax.experimental import pallas as pl
from jax.experimental.pallas import tpu as pltpu

D_MODEL = 2048
DEPTH = 2
NSA_HEADS = 16
NSA_KV_GROUPS = 4
NSA_HEAD_DIM = 64
NSA_CMP_LEN = 32
NSA_CMP_STRIDE = 16
NSA_CMP_HIDDEN = 256
NSA_SEL_LEN = 64
NSA_SEL_BLOCKS = 16
NSA_WINDOW = 512
DSA_HEADS = 16
DSA_HEAD_DIM = 64
DSA_KV_LATENT = 256
IDX_HEADS = 16
IDX_DIM = 32
DSA_TOPK_MAX = 256
INDEX_SCALE = (IDX_HEADS * IDX_DIM) ** -0.5
D_FF = 5632
REL_BUCKETS = 32
REL_MAX_DIST = 128
Q_BLOCK = 128
N_SUBLAYERS = 3
RMS_EPS = 1e-6
NEG_INF = -1e30
FORCE_BONUS = 1e4
NSA_KV_WIDTH = NSA_KV_GROUPS * NSA_HEAD_DIM

MXU_DTYPE = jnp.bfloat16
VMEM_LIMIT_BYTES = 56 * 1024 * 1024
LANES = 128
KEY_TILE = 512
TILE_BLOCKS = KEY_TILE // Q_BLOCK
TILE_SEL_BLOCKS = KEY_TILE // NSA_SEL_LEN
SEL_SHIFT = NSA_SEL_LEN.bit_length() - 1

COL_AM = 0
COL_QN = 4096
COL_QD = 5120
COL_KC, COL_VC, COL_KS, COL_VS, COL_KW, COL_VW = 6144, 6400, 6656, 6912, 7168, 7424
COL_QI = 7680
COL_CKV = 8192
COL_SMALL = 8448
N_IN_PAD = 8704


def _dot(a, b):
    return jnp.dot(a, b, preferred_element_type=jnp.float32)


def _rms_scale(x):
    return x * lax.rsqrt(jnp.mean(x * x, axis=-1, keepdims=True) + RMS_EPS)


def _modulated_norm(x, g, shift, scale):
    return (_rms_scale(x) * g) * (1.0 + scale) + shift


def _split_hi_lo(x):
    hi = x.astype(MXU_DTYPE)
    return hi, (x - hi.astype(jnp.float32)).astype(MXU_DTYPE)


def _adaln_kernel(c_ref, w_ref, b_ref, o_ref):
    c = c_ref[...]
    c_hi, c_lo = _split_hi_lo(c * jax.nn.sigmoid(c))
    w_hi, w_lo = _split_hi_lo(w_ref[0])
    o_ref[0] = _dot(c_hi, w_hi) + _dot(c_hi, w_lo) + _dot(c_lo, w_hi) + b_ref[0]


def _adaln(c, w_ada, b_ada):
    depth, d, n = w_ada.shape
    rows = 16
    c = jnp.pad(c, ((0, rows - c.shape[0]), (0, 0)))
    b = rows
    tn = 1024
    return pl.pallas_call(
        _adaln_kernel,
        grid=(depth, n // tn),
        in_specs=[
            pl.BlockSpec((b, d), lambda l, j: (0, 0)),
            pl.BlockSpec((1, d, tn), lambda l, j: (l, 0, j)),
            pl.BlockSpec((1, 1, tn), lambda l, j: (l, 0, j)),
        ],
        out_specs=pl.BlockSpec((1, b, tn), lambda l, j: (l, 0, j)),
        out_shape=jax.ShapeDtypeStruct((depth, b, n), jnp.float32),
        compiler_params=pltpu.CompilerParams(
            dimension_semantics=("arbitrary", "arbitrary"), vmem_limit_bytes=VMEM_LIMIT_BYTES),
        name="adaln",
    )(c, w_ada, b_ada.reshape(depth, 1, n))


def _ffn_kernel(x_ref, shift_ref, scale_ref, gate_ref, g_ref, wg_ref, wu_ref, wo_ref, gfin_ref, o_ref,
                h_scr, acc_scr, *, final_norm):
    j = pl.program_id(1)

    @pl.when(j == 0)
    def _():
        h = _modulated_norm(x_ref[...], g_ref[...], shift_ref[0], scale_ref[0])
        h_scr[...] = h.astype(h_scr.dtype)
        acc_scr[...] = jnp.zeros_like(acc_scr)

    h = h_scr[...]
    g = _dot(h, wg_ref[...])
    u = _dot(h, wu_ref[...])
    a = (g * jax.nn.sigmoid(g)) * u
    acc_scr[...] += _dot(a.astype(wo_ref.dtype), wo_ref[...])

    @pl.when(j == pl.num_programs(1) - 1)
    def _():
        y = x_ref[...] + (0.5 * gate_ref[0]) * acc_scr[...]
        if final_norm:
            y = _rms_scale(y) * gfin_ref[...]
        o_ref[...] = y


def _ffn(x2d, shift, scale, gate, g, w_i, w_o, g_final, *, seq, final_norm):
    n, d = x2d.shape
    tm, tf = 512, 512
    nf = D_FF // tf
    per_b = seq // tm
    mod_spec = pl.BlockSpec((1, 1, d), lambda i, j: (i // per_b, 0, 0))
    return pl.pallas_call(
        functools.partial(_ffn_kernel, final_norm=final_norm),
        grid=(n // tm, nf),
        in_specs=[
            pl.BlockSpec((tm, d), lambda i, j: (i, 0)),
            mod_spec, mod_spec, mod_spec,
            pl.BlockSpec((1, d), lambda i, j: (0, 0)),
            pl.BlockSpec((d, tf), lambda i, j: (0, j)),
            pl.BlockSpec((d, tf), lambda i, j: (0, j + nf)),
            pl.BlockSpec((tf, d), lambda i, j: (j, 0)),
            pl.BlockSpec((1, d), lambda i, j: (0, 0)),
        ],
        out_specs=pl.BlockSpec((tm, d), lambda i, j: (i, 0)),
        out_shape=jax.ShapeDtypeStruct((n, d), jnp.float32),
        scratch_shapes=[pltpu.VMEM((tm, d), MXU_DTYPE), pltpu.VMEM((tm, d), jnp.float32)],
        compiler_params=pltpu.CompilerParams(
            dimension_semantics=("arbitrary", "arbitrary"), vmem_limit_bytes=VMEM_LIMIT_BYTES),
        name="ffn",
    )(x2d, shift, scale, gate, g.reshape(1, d), w_i, w_i, w_o, g_final.reshape(1, d))


def _inproj_kernel(x_ref, shift_ref, scale_ref, g_ref, w_ref, o_ref, h_scr):
    @pl.when(pl.program_id(1) == 0)
    def _():
        h = _modulated_norm(x_ref[...], g_ref[...], shift_ref[0], scale_ref[0])
        h_scr[...] = h.astype(h_scr.dtype)

    o_ref[...] = _dot(h_scr[...], w_ref[...]).astype(o_ref.dtype)


def _inproj(x2d, shift, scale, g, w, *, seq):
    n, d = x2d.shape
    tm, tn = 512, N_IN_PAD // 4
    per_b = seq // tm
    mod_spec = pl.BlockSpec((1, 1, d), lambda i, j: (i // per_b, 0, 0))
    return pl.pallas_call(
        _inproj_kernel,
        grid=(n // tm, N_IN_PAD // tn),
        in_specs=[
            pl.BlockSpec((tm, d), lambda i, j: (i, 0)),
            mod_spec, mod_spec,
            pl.BlockSpec((1, d), lambda i, j: (0, 0)),
            pl.BlockSpec((d, tn), lambda i, j: (0, j)),
        ],
        out_specs=pl.BlockSpec((tm, tn), lambda i, j: (i, j)),
        out_shape=jax.ShapeDtypeStruct((n, N_IN_PAD), MXU_DTYPE),
        scratch_shapes=[pltpu.VMEM((tm, d), MXU_DTYPE)],
        compiler_params=pltpu.CompilerParams(
            dimension_semantics=("arbitrary", "arbitrary"), vmem_limit_bytes=VMEM_LIMIT_BYTES),
        name="inproj",
    )(x2d, shift, scale, g.reshape(1, d), w)


def _merge_kernel(x_ref, gate_ref, on_ref, od_ref, a0_ref, a1_ref, wun_ref, wud_ref, wout_ref, o_ref):
    yn = _dot(on_ref[...], wun_ref[...])
    yd = _dot(od_ref[...], wud_ref[...])
    a0 = jax.nn.sigmoid(a0_ref[...].astype(jnp.float32))
    a1 = jax.nn.sigmoid(a1_ref[...].astype(jnp.float32))
    y = a0 * yn + a1 * yd
    o_ref[...] = x_ref[...] + gate_ref[0] * _dot(y.astype(wout_ref.dtype), wout_ref[...])


def _merge(x2d, gate, o_n, o_d, proj, w_up_nsa, w_up_dsa, w_out, *, seq):
    n, d = x2d.shape
    tm = 512
    per_b = seq // tm
    hw = o_n.shape[1]
    const = dict(pipeline_mode=pl.Buffered(1))
    return pl.pallas_call(
        _merge_kernel,
        grid=(n // tm,),
        in_specs=[
            pl.BlockSpec((tm, d), lambda i: (i, 0)),
            pl.BlockSpec((1, 1, d), lambda i: (i // per_b, 0, 0)),
            pl.BlockSpec((tm, hw), lambda i: (i, 0)),
            pl.BlockSpec((tm, hw), lambda i: (i, 0)),
            pl.BlockSpec((tm, d), lambda i: (i, COL_AM // d)),
            pl.BlockSpec((tm, d), lambda i: (i, COL_AM // d + 1)),
            pl.BlockSpec((hw, d), lambda i: (0, 0), **const),
            pl.BlockSpec((hw, d), lambda i: (0, 0), **const),
            pl.BlockSpec((d, d), lambda i: (0, 0), **const),
        ],
        out_specs=pl.BlockSpec((tm, d), lambda i: (i, 0)),
        out_shape=jax.ShapeDtypeStruct((n, d), jnp.float32),
        compiler_params=pltpu.CompilerParams(
            dimension_semantics=("arbitrary",), vmem_limit_bytes=VMEM_LIMIT_BYTES),
        name="merge",
    )(x2d, gate, o_n, o_d, proj, proj, w_up_nsa, w_up_dsa, w_out)


def _dot_nt(a, b):
    return lax.dot_general(a, b, (((1,), (1,)), ((), ())), preferred_element_type=jnp.float32)


def _iota(shape, dim):
    return lax.broadcasted_iota(jnp.int32, shape, dim)


def _softmax_rows(s, mask):
    s = jnp.where(mask, s, NEG_INF)
    m = jnp.max(s, axis=1, keepdims=True)
    p = jnp.where(mask, jnp.exp(s - m), 0.0)
    return p * (1.0 / jnp.maximum(jnp.sum(p, axis=1, keepdims=True), 1e-30))


def _lane_tiles(x):
    return [x[:, u * LANES:(u + 1) * LANES] for u in range(x.shape[1] // LANES)]


def _max_pass(s_parts, mask, rows0, mx_scr):
    qb = s_parts[0].shape[0]
    for r, s in enumerate(s_parts):
        tiles = _lane_tiles(jnp.where(mask, s, NEG_INF))
        red = functools.reduce(jnp.maximum, tiles)
        sl = slice(rows0 + r * qb, rows0 + (r + 1) * qb)
        mx_scr[sl, :] = jnp.maximum(mx_scr[sl, :], red)


def _exp_pass(s_parts, mask, shifts, v_tile, rows0, ls_scr, acc_scr):
    qb = s_parts[0].shape[0]
    ps = []
    for r, (s, shift) in enumerate(zip(s_parts, shifts)):
        if mask is not None:
            s = jnp.where(mask, s, NEG_INF)
        ptiles = [jnp.exp(t - shift) for t in _lane_tiles(s)]
        sl = slice(rows0 + r * qb, rows0 + (r + 1) * qb)
        ls_scr[sl, :] = ls_scr[sl, :] + functools.reduce(jnp.add, ptiles)
        ps.append(jnp.concatenate(ptiles, axis=1))
    p = jnp.concatenate(ps, axis=0).astype(v_tile.dtype)
    rows = slice(rows0, rows0 + len(s_parts) * qb)
    acc_scr[rows, :] = acc_scr[rows, :] + _dot(p, v_tile)


def _compress_kernel(x_ref, pe_ref, w1_ref, w2_ref, o_ref, *, transposed):
    x = x_ref[0, 0]
    half = x.shape[1]
    first = _dot(x, w1_ref[0:half, :])
    second = _dot(x, w1_ref[half:2 * half, :])
    pe_term = _dot(pe_ref[...], w1_ref[...])[0:1]
    hid = first + pltpu.roll(second, x.shape[0] - 1, 0) + pe_term
    hid = (hid * jax.nn.sigmoid(hid)).astype(w2_ref.dtype)
    if transposed:
        o_ref[0, 0] = _dot_nt(w2_ref[...], hid).astype(o_ref.dtype)
    else:
        o_ref[0, 0] = _dot(hid, w2_ref[...]).astype(o_ref.dtype)


def _compress(x_chunks, pe, w1, w2, *, transposed):
    b, g, nc, width = x_chunks.shape
    dk = NSA_HEAD_DIM
    pe8 = jnp.zeros((8, 2 * width), MXU_DTYPE).at[0].set(pe.reshape(-1).astype(MXU_DTYPE))
    w2m = (w2.T if transposed else w2).astype(MXU_DTYPE)
    oshape = (b, g, dk, nc) if transposed else (b, g, nc, dk)
    return pl.pallas_call(
        functools.partial(_compress_kernel, transposed=transposed),
        grid=(b, g),
        in_specs=[
            pl.BlockSpec((1, 1, nc, width), lambda i, j: (i, j, 0, 0)),
            pl.BlockSpec((8, 2 * width), lambda i, j: (0, 0)),
            pl.BlockSpec((2 * width, NSA_CMP_HIDDEN), lambda i, j: (0, 0)),
            pl.BlockSpec(w2m.shape, lambda i, j: (0, 0)),
        ],
        out_specs=pl.BlockSpec((1, 1) + oshape[2:], lambda i, j: (i, j, 0, 0)),
        out_shape=jax.ShapeDtypeStruct(oshape, MXU_DTYPE),
        compiler_params=pltpu.CompilerParams(dimension_semantics=("arbitrary", "arbitrary")),
        name="nsa_compress",
    )(x_chunks, pe8, w1.astype(MXU_DTYPE), w2m)


def _ckvnorm_kernel(x_ref, g_ref, o_ref):
    x = x_ref[...].astype(jnp.float32)
    o_ref[...] = (_rms_scale(x) * g_ref[...]).astype(o_ref.dtype)


def _ckvnorm(proj, g_kv):
    n = proj.shape[0]
    tm = 2048
    return pl.pallas_call(
        _ckvnorm_kernel,
        grid=(n // tm,),
        in_specs=[pl.BlockSpec((tm, DSA_KV_LATENT), lambda i: (i, COL_CKV // DSA_KV_LATENT)),
                  pl.BlockSpec((1, DSA_KV_LATENT), lambda i: (0, 0))],
        out_specs=pl.BlockSpec((tm, DSA_KV_LATENT), lambda i: (i, 0)),
        out_shape=jax.ShapeDtypeStruct((n, DSA_KV_LATENT), MXU_DTYPE),
        compiler_params=pltpu.CompilerParams(dimension_semantics=("arbitrary",)),
        name="ckv_norm",
    )(proj, g_kv.reshape(1, -1).astype(jnp.float32))


def _nsa_kernel(far_ref, q_ref, small_ref, kct_ref, vc_ref, kst_ref, vs_ref, kwt_ref, vw_ref, bw_ref, bc_ref,
                ovt_ref, o_ref, qs_scr, sel_scr, mxf_scr, mb_scr, ls_scr, acc_scr, sn_scr, oc_scr, ow_scr):
    i = pl.program_id(1)
    t0 = i * Q_BLOCK
    qb, dk, hg = Q_BLOCK, NSA_HEAD_DIM, NSA_HEADS // NSA_KV_GROUPS
    grows = hg * qb
    nc = kct_ref.shape[2]
    nb = ovt_ref.shape[0]
    wk = qb + NSA_WINDOW
    f32 = jnp.float32

    gates = jax.nn.sigmoid(small_ref[0].astype(f32))
    n_io = _iota((qb, nc), 1)
    tc_io = t0 + _iota((qb, nc), 0)
    mask_c = NSA_CMP_STRIDE * n_io + (NSA_CMP_LEN - 1) <= tc_io
    near_lo = (qb // NSA_CMP_STRIDE) * i - 16
    near_c = n_io >= near_lo
    shift_c = jnp.where(near_lo >= 0, near_lo, near_lo + nc)
    j_io = _iota((nb, qb), 0)
    tj_io = t0 + _iota((nb, qb), 1)
    cur = jnp.right_shift(tj_io, SEL_SHIFT)
    forced = (j_io == 0) | (j_io == cur) | (j_io == cur - 1)
    visible = j_io * NSA_SEL_LEN <= tj_io
    eye = jnp.where(_iota((qb, qb), 0) == _iota((qb, qb), 1), 1.0, 0.0).astype(MXU_DTYPE)
    kw_io = _iota((qb, wk), 1)
    off_w = _iota((qb, wk), 0) - kw_io + NSA_WINDOW
    mask_w = (off_w >= 0) & (off_w < NSA_WINDOW) & (t0 - NSA_WINDOW + kw_io >= 0)
    kn_io = _iota((qb, 2 * qb), 1)
    pos_n = t0 - qb + kn_io
    causal_n = (pos_n <= t0 + _iota((qb, 2 * qb), 0)) & (pos_n >= 0)
    n_far = (i - 1 + TILE_BLOCKS - 1) // TILE_BLOCKS
    far_blocks = (qb // NSA_SEL_LEN) * (i - 1)
    jsel_io = _iota((qb, nb), 1)
    qs_scr[...] = q_ref[0, 0] * (dk ** -0.5)

    for g in range(NSA_KV_GROUPS):
        rows = pl.ds(g * grows, grows)
        gsl = slice(g * dk, (g + 1) * dk)
        qg = qs_scr[g * grows:(g + 1) * grows, :]

        s = _dot(qg, kct_ref[0, gsl, :])
        probs = []
        for r in range(hg):
            h = g * hg + r
            bias = jnp.where(near_c, pltpu.roll(bc_ref[h], shift_c, 1), far_ref[h])
            probs.append(_softmax_rows(s[r * qb:(r + 1) * qb] + bias, mask_c))
        pstack = jnp.concatenate(probs, axis=0)
        oc_scr[rows, :] = _dot(pstack.astype(vc_ref.dtype), vc_ref[0])
        pg = probs[0] + probs[1] + probs[2] + probs[3]
        hi = pg.astype(MXU_DTYPE)
        r1 = pg - hi.astype(f32)
        mid = r1.astype(MXU_DTYPE)
        lo = (r1 - mid.astype(f32)).astype(MXU_DTYPE)
        ovt = ovt_ref[...]
        imp = _dot_nt(ovt, hi) + _dot_nt(ovt, mid) + _dot_nt(ovt, lo)
        imp = jnp.where(visible, imp + jnp.where(forced, FORCE_BONUS, 0.0), NEG_INF)
        ranks = [jnp.zeros((nb, qb), f32) for _ in range(4)]
        for jp in range(nb):
            row = imp[jp:jp + 1, :]
            beats = jnp.where(row > imp, 1.0, jnp.where(row == imp, jnp.where(jp < j_io, 1.0, 0.0), 0.0))
            ranks[jp % 4] = ranks[jp % 4] + beats
        rank = (ranks[0] + ranks[1]) + (ranks[2] + ranks[3])
        sel_t = jnp.where(rank < NSA_SEL_BLOCKS, 1.0, 0.0).astype(MXU_DTYPE)
        sel = _dot_nt(eye, sel_t)
        sel_scr[2 * g] = sel
        sel_scr[2 * g + 1] = jnp.where(jsel_io < far_blocks, sel, 0.0)

        kwt = jnp.concatenate([kwt_ref[0, i + u, gsl, :] for u in range(wk // qb)], axis=1)
        s = _dot(qg, kwt)
        probs, sums = [], []
        for r in range(hg):
            sh = jnp.where(mask_w, s[r * qb:(r + 1) * qb] + bw_ref[g * hg + r], NEG_INF)
            p = jnp.exp(sh - jnp.max(sh, axis=1, keepdims=True))
            probs.append(p)
            sums.append(jnp.sum(p, axis=1, keepdims=True))
        pstack = jnp.concatenate(probs, axis=0)
        ow = _dot(pstack.astype(vw_ref.dtype), vw_ref[0, pl.ds(pl.multiple_of(t0, qb), wk), :])
        for r in range(hg):
            ow_scr[g * grows + r * qb:g * grows + (r + 1) * qb, :] = ow[r * qb:(r + 1) * qb] * (1.0 / sums[r])

    def far_tile(jt, g):
        gsl = slice(g * dk, (g + 1) * dk)
        kt = jnp.concatenate([kst_ref[0, 1 + TILE_BLOCKS * jt + u, gsl, :] for u in range(TILE_BLOCKS)], axis=1)
        expand = jnp.where(_iota((nb, KEY_TILE), 0)
                           == TILE_SEL_BLOCKS * jt + jnp.right_shift(_iota((nb, KEY_TILE), 1), SEL_SHIFT),
                           1.0, 0.0).astype(MXU_DTYPE)
        mask = _dot(sel_scr[2 * g + 1].astype(MXU_DTYPE), expand) > 0.5
        sc = _dot(qs_scr[g * grows:(g + 1) * grows, :], kt)
        return [sc[r * qb:(r + 1) * qb] for r in range(hg)], mask

    mxf_scr[...] = jnp.full(mxf_scr.shape, NEG_INF, f32)
    ls_scr[...] = jnp.zeros(ls_scr.shape, f32)
    acc_scr[...] = jnp.zeros(acc_scr.shape, f32)

    def far_max_body(jt):
        for g in range(NSA_KV_GROUPS):
            parts, mask = far_tile(jt, g)
            _max_pass(parts, mask, g * grows, mxf_scr)

    _for_each_tile_paired(n_far, far_max_body)

    expand_n = jnp.where(_iota((nb, 2 * qb), 0)
                         == far_blocks + jnp.right_shift(_iota((nb, 2 * qb), 1), SEL_SHIFT),
                         1.0, 0.0).astype(MXU_DTYPE)
    for g in range(NSA_KV_GROUPS):
        gsl = slice(g * dk, (g + 1) * dk)
        kt = jnp.concatenate([kst_ref[0, i + u, gsl, :] for u in range(2)], axis=1)
        mask = (_dot(sel_scr[2 * g].astype(MXU_DTYPE), expand_n) > 0.5) & causal_n
        sc = _dot(qs_scr[g * grows:(g + 1) * grows, :], kt)
        for r in range(hg):
            h = g * hg + r
            hr = slice(g * grows + r * qb, g * grows + (r + 1) * qb)
            sn = jnp.where(mask, sc[r * qb:(r + 1) * qb] + bw_ref[h][:, NSA_WINDOW - qb:], NEG_INF)
            sn_scr[hr, :] = sn
            m_far = jnp.max(mxf_scr[hr, :], axis=1, keepdims=True) + far_ref[h]
            m_row = jnp.maximum(m_far, jnp.max(sn, axis=1, keepdims=True))
            mb_scr[hr, :] = jnp.broadcast_to(m_row, (qb, LANES))

    def far_exp_body(jt):
        vt = vs_ref[0, pl.ds(pl.multiple_of(qb + KEY_TILE * jt, qb), KEY_TILE), :]
        for g in range(NSA_KV_GROUPS):
            parts, mask = far_tile(jt, g)
            shifts = [mb_scr[g * grows + r * qb:g * grows + (r + 1) * qb, :] - far_ref[g * hg + r]
                      for r in range(hg)]
            _exp_pass(parts, mask, shifts, vt, g * grows, ls_scr, acc_scr)

    _for_each_tile_paired(n_far, far_exp_body)

    vt_n = vs_ref[0, pl.ds(pl.multiple_of(t0, qb), 2 * qb), :]
    for g in range(NSA_KV_GROUPS):
        hrs = [slice(g * grows + r * qb, g * grows + (r + 1) * qb) for r in range(hg)]
        _exp_pass([sn_scr[hr, :] for hr in hrs], None, [mb_scr[hr, :] for hr in hrs], vt_n, g * grows,
                  ls_scr, acc_scr)

    for g in range(NSA_KV_GROUPS):
        slab = slice((g // 2) * LANES, (g // 2) * LANES + LANES)
        placed = []
        for r in range(hg):
            h = g * hg + r
            hr = slice(g * grows + r * qb, g * grows + (r + 1) * qb)
            o_s = acc_scr[hr, slab] * (1.0 / jnp.sum(ls_scr[hr, :], axis=1, keepdims=True))
            o_h = (gates[:, 3 * h:3 * h + 1] * oc_scr[hr, slab] + gates[:, 3 * h + 1:3 * h + 2] * o_s
                   + gates[:, 3 * h + 2:3 * h + 3] * ow_scr[hr, slab])
            if g % 2 != r % 2:
                o_h = pltpu.roll(o_h, dk, 1)
            placed.append(o_h)
        low_half = _iota((qb, LANES), 1) < dk
        for pr in range(hg // 2):
            pair = jnp.where(low_half, placed[2 * pr], placed[2 * pr + 1])
            col = (g * hg // 2 + pr) * LANES
            o_ref[0, :, col:col + LANES] = pair.astype(o_ref.dtype)


def _nsa(proj3, q_r, kct, vc, kst, vs, kwt, vw, bias_w, bias_c, ovt, far):
    b, s, _ = proj3.shape
    nq = s // Q_BLOCK
    rows = NSA_HEADS * Q_BLOCK
    vdim = NSA_KV_WIDTH
    const = dict(pipeline_mode=pl.Buffered(1))

    def whole(a):
        nd = a.ndim
        return pl.BlockSpec((1,) + a.shape[1:], lambda bi, i: (bi,) + (0,) * (nd - 1))

    return pl.pallas_call(
        _nsa_kernel,
        grid=(b, nq),
        in_specs=[
            pl.BlockSpec(memory_space=pltpu.SMEM),
            pl.BlockSpec((1, 1, rows, NSA_HEAD_DIM), lambda bi, i: (bi, i, 0, 0)),
            pl.BlockSpec((1, Q_BLOCK, 128), lambda bi, i: (bi, i, COL_SMALL // 128)),
            whole(kct), whole(vc), whole(kst), whole(vs), whole(kwt), whole(vw),
            pl.BlockSpec(bias_w.shape, lambda bi, i: (0, 0, 0), **const),
            pl.BlockSpec(bias_c.shape, lambda bi, i: (0, 0, 0), **const),
            pl.BlockSpec(ovt.shape, lambda bi, i: (0, 0), **const),
        ],
        out_specs=pl.BlockSpec((1, Q_BLOCK, NSA_HEADS * NSA_HEAD_DIM), lambda bi, i: (bi, i, 0)),
        out_shape=jax.ShapeDtypeStruct((b, s, NSA_HEADS * NSA_HEAD_DIM), MXU_DTYPE),
        scratch_shapes=[pltpu.VMEM((rows, NSA_HEAD_DIM), MXU_DTYPE),
                        pltpu.VMEM((2 * NSA_KV_GROUPS, Q_BLOCK, ovt.shape[0]), jnp.float32),
                        pltpu.VMEM((rows, LANES), jnp.float32),
                        pltpu.VMEM((rows, LANES), jnp.float32),
                        pltpu.VMEM((rows, LANES), jnp.float32),
                        pltpu.VMEM((rows, vdim), jnp.float32),
                        pltpu.VMEM((rows, 2 * Q_BLOCK), jnp.float32),
                        pltpu.VMEM((rows, vdim), jnp.float32),
                        pltpu.VMEM((rows, vdim), jnp.float32)],
        compiler_params=pltpu.CompilerParams(
            dimension_semantics=("arbitrary", "arbitrary"), vmem_limit_bytes=VMEM_LIMIT_BYTES),
        name="nsa_attention",
    )(far, q_r, proj3, kct, vc, kst, vs, kwt, vw, bias_w, bias_c, ovt)


INT_MIN = -2 ** 31
DSA_ATT_HEADS_PER_CHUNK = 4


def _for_each_tile_paired(n, body):
    def pair(p, carry):
        body(2 * p)
        body(2 * p + 1)
        return carry

    lax.fori_loop(0, n // 2, pair, 0)

    @pl.when(n % 2 == 1)
    def _():
        body(n - 1)


def _sortable_key(x):
    bits = pltpu.bitcast(x, jnp.int32)
    return bits ^ (jnp.right_shift(bits, 31) & 0x7FFFFFFF)


def _dsa_kernel(far_ref, qd_ref, qit_ref, wt_ref, ki_ref, ct_ref, c_ref, wuk_ref, wuv_ref, bd_ref, o_ref,
                qlat_scr, keyt_scr, key_scr, mxf_scr, mb_scr, ls_scr, acc_scr, sn_scr, tie_scr):
    i = pl.program_id(1)
    t0 = i * Q_BLOCK
    qb = Q_BLOCK
    f32 = jnp.float32
    k_top = float(DSA_TOPK_MAX)
    chunk = 4 * qb
    n_chunks = DSA_HEADS // 4

    for pr in range(DSA_HEADS // 2):
        r = _dot(qd_ref[0, :, pr * 128:(pr + 1) * 128], wuk_ref[pr]) * (DSA_HEAD_DIM ** -0.5)
        qlat_scr[2 * pr * qb:(2 * pr + 1) * qb, :] = r[:, :DSA_KV_LATENT].astype(qlat_scr.dtype)
        qlat_scr[(2 * pr + 1) * qb:(2 * pr + 2) * qb, :] = r[:, DSA_KV_LATENT:].astype(qlat_scr.dtype)

    n_tiles = i // TILE_BLOCKS + 1
    w_t = wt_ref[0, 0].astype(f32) * INDEX_SCALE
    tq_io = t0 + _iota((KEY_TILE, qb), 1)
    keyt_scr[0:qb, :] = jnp.full((qb, qb), INT_MIN, jnp.int32)

    def tile_rows(jt):
        return pl.ds(pl.multiple_of(qb + KEY_TILE * jt, qb), KEY_TILE)

    def index_body(jt):
        kt = ki_ref[0, pl.ds(pl.multiple_of(KEY_TILE * jt, KEY_TILE), KEY_TILE), :]
        sc = jnp.zeros((KEY_TILE, qb), f32)
        for hc in range(n_chunks):
            lg = _dot(kt, qit_ref[0, 0, :, hc * chunk:(hc + 1) * chunk])
            for r in range(4):
                h = hc * 4 + r
                sc = sc + jnp.maximum(lg[:, r * qb:(r + 1) * qb], 0.0) * w_t[h:h + 1, :]
        sc = jnp.where(KEY_TILE * jt + _iota((KEY_TILE, qb), 0) <= tq_io, sc, NEG_INF)
        keyt_scr[tile_rows(jt), :] = _sortable_key(sc)

    _for_each_tile_paired(n_tiles, index_body)

    def count(pred):
        def body(jt, c):
            x = jnp.where(pred(keyt_scr[tile_rows(jt), :], KEY_TILE * jt + _iota((KEY_TILE, qb), 0)), 1.0, 0.0)
            parts = [x[8 * v:8 * v + 8] for v in range(KEY_TILE // 8)]
            while len(parts) > 1:
                parts = [a + b for a, b in zip(parts[0::2], parts[1::2])]
            return c + parts[0]
        c = lax.fori_loop(0, n_tiles // 2, lambda p, c: body(2 * p + 1, body(2 * p, c)), jnp.zeros((8, qb), f32))
        c = lax.fori_loop(2 * (n_tiles // 2), n_tiles, body, c)
        return jnp.sum(c, axis=0, keepdims=True)

    thr = jnp.where(count(lambda kk, pos: kk >= 0) >= k_top, 0, INT_MIN).astype(jnp.int32)
    thr = thr + jnp.zeros((1, qb), jnp.int32)

    def bit_body(bit, thr):
        cand = thr | jnp.left_shift(jnp.int32(1), 30 - bit)
        return jnp.where(count(lambda kk, pos: kk >= cand) >= k_top, cand, thr)

    thr = lax.fori_loop(0, 31, bit_body, thr)
    need = k_top - count(lambda kk, pos: kk > thr)
    n_ge = count(lambda kk, pos: kk >= thr)
    tie_scr[...] = jnp.full((1, qb), 2 ** 30, jnp.int32)
    has_excess = jnp.max(jnp.where(n_ge > k_top, 1.0, 0.0)) > 0.5

    @pl.when(has_excess)
    def _():
        def pos_body(bit, last):
            cand = last | jnp.left_shift(jnp.int32(1), 12 - bit)
            n_before = count(lambda kk, pos: (kk == thr) & (pos < cand))
            return jnp.where(n_before < need, cand, last)
        tie_scr[...] = lax.fori_loop(0, 13, pos_body, jnp.zeros((1, qb), jnp.int32))

    last_tie = tie_scr[...]
    eye = jnp.where(_iota((qb, qb), 0) == _iota((qb, qb), 1), 1.0, 0.0).astype(MXU_DTYPE)

    def selected_rows(kk, pos, extra=None):
        chosen = (kk > thr) | ((kk == thr) & (pos <= last_tie))
        if extra is not None:
            chosen = chosen & extra
        return _dot_nt(eye, jnp.where(chosen, 1.0, 0.0).astype(MXU_DTYPE)) > 0.5

    pos_n = t0 - qb + _iota((2 * qb, qb), 0)
    causal_n = (pos_n <= t0 + _iota((2 * qb, qb), 1)) & (pos_n >= 0)
    mask_n = selected_rows(keyt_scr[pl.ds(pl.multiple_of(t0, qb), 2 * qb), :], pos_n, causal_n)

    def mask_body(jt):
        mt = selected_rows(keyt_scr[tile_rows(jt), :], KEY_TILE * jt + _iota((KEY_TILE, qb), 0))
        for u in range(TILE_BLOCKS):
            bi = TILE_BLOCKS * jt + u
            key_scr[1 + bi] = jnp.where(mt[:, u * qb:(u + 1) * qb], jnp.where(bi < i - 1, 1, 0), 0)

    _for_each_tile_paired(n_tiles, mask_body)

    mxf_scr[...] = jnp.full(mxf_scr.shape, NEG_INF, f32)
    ls_scr[...] = jnp.zeros(ls_scr.shape, f32)
    acc_scr[...] = jnp.zeros(acc_scr.shape, f32)
    n_far = (i - 1 + TILE_BLOCKS - 1) // TILE_BLOCKS

    hpc = DSA_ATT_HEADS_PER_CHUNK
    arows = hpc * qb
    n_att = DSA_HEADS // hpc

    def far_tile(jt, hc):
        ct = jnp.concatenate([ct_ref[0, 1 + TILE_BLOCKS * jt + u] for u in range(TILE_BLOCKS)], axis=1)
        sc = _dot(qlat_scr[hc * arows:(hc + 1) * arows, :], ct)
        return [sc[r * qb:(r + 1) * qb] for r in range(hpc)]

    def far_mask(jt):
        return jnp.concatenate([key_scr[1 + TILE_BLOCKS * jt + u] for u in range(TILE_BLOCKS)], axis=1) > 0

    def far_max_body(jt):
        mask = far_mask(jt)
        for hc in range(n_att):
            _max_pass(far_tile(jt, hc), mask, hc * arows, mxf_scr)

    _for_each_tile_paired(n_far, far_max_body)

    ct_n = jnp.concatenate([ct_ref[0, i], ct_ref[0, i + 1]], axis=1)
    for hc in range(n_att):
        sc = _dot(qlat_scr[hc * arows:(hc + 1) * arows, :], ct_n)
        for r in range(hpc):
            h = hc * hpc + r
            hr = slice(h * qb, (h + 1) * qb)
            sn = jnp.where(mask_n, sc[r * qb:(r + 1) * qb] + bd_ref[h], NEG_INF)
            sn_scr[hr, :] = sn
            m_far = jnp.max(mxf_scr[hr, :], axis=1, keepdims=True) + far_ref[NSA_HEADS + h]
            m_row = jnp.maximum(m_far, jnp.max(sn, axis=1, keepdims=True))
            mb_scr[hr, :] = jnp.broadcast_to(m_row, (qb, LANES))

    def far_exp_body(jt):
        mask = far_mask(jt)
        cv = c_ref[0, pl.ds(pl.multiple_of(qb + KEY_TILE * jt, qb), KEY_TILE), :]
        for hc in range(n_att):
            shifts = [mb_scr[(hc * hpc + r) * qb:(hc * hpc + r + 1) * qb, :] - far_ref[NSA_HEADS + hc * hpc + r]
                      for r in range(hpc)]
            _exp_pass(far_tile(jt, hc), mask, shifts, cv, hc * arows, ls_scr, acc_scr)

    _for_each_tile_paired(n_far, far_exp_body)

    cv_n = c_ref[0, pl.ds(pl.multiple_of(t0, qb), 2 * qb), :]
    for hc in range(n_att):
        hrs = [slice((hc * hpc + r) * qb, (hc * hpc + r + 1) * qb) for r in range(hpc)]
        _exp_pass([sn_scr[hr, :] for hr in hrs], None, [mb_scr[hr, :] for hr in hrs], cv_n, hc * arows,
                  ls_scr, acc_scr)

    def normalised(hr):
        return acc_scr[hr, :] * (1.0 / jnp.sum(ls_scr[hr, :], axis=1, keepdims=True))

    for pr in range(DSA_HEADS // 2):
        ra = slice(2 * pr * qb, (2 * pr + 1) * qb)
        rb = slice((2 * pr + 1) * qb, (2 * pr + 2) * qb)
        o_lat = jnp.concatenate([normalised(ra), normalised(rb)], axis=1)
        o_ref[0, :, pr * 128:(pr + 1) * 128] = _dot(o_lat.astype(wuv_ref.dtype), wuv_ref[pr]).astype(o_ref.dtype)


def _dsa(proj3, qi_t, w_t, ki, ct, c_pad, wuk_bd, wuv_bd, bias_d, far):
    b, s, _ = proj3.shape
    nq = s // Q_BLOCK
    rows = DSA_HEADS * Q_BLOCK
    n_tiles = -(-nq // TILE_BLOCKS)
    const = dict(pipeline_mode=pl.Buffered(1))

    def whole(a):
        nd = a.ndim
        return pl.BlockSpec((1,) + a.shape[1:], lambda bi, i: (bi,) + (0,) * (nd - 1))

    return pl.pallas_call(
        _dsa_kernel,
        grid=(b, nq),
        in_specs=[
            pl.BlockSpec(memory_space=pltpu.SMEM),
            pl.BlockSpec((1, Q_BLOCK, DSA_HEADS * DSA_HEAD_DIM), lambda bi, i: (bi, i, COL_QD // 1024)),
            pl.BlockSpec((1, 1, IDX_DIM, rows), lambda bi, i: (bi, i, 0, 0)),
            pl.BlockSpec((1, 1, IDX_HEADS, Q_BLOCK), lambda bi, i: (bi, i, 0, 0)),
            whole(ki), whole(ct), whole(c_pad),
            pl.BlockSpec(wuk_bd.shape, lambda bi, i: (0, 0, 0), **const),
            pl.BlockSpec(wuv_bd.shape, lambda bi, i: (0, 0, 0), **const),
            pl.BlockSpec(bias_d.shape, lambda bi, i: (0, 0, 0), **const),
        ],
        out_specs=pl.BlockSpec((1, Q_BLOCK, DSA_HEADS * DSA_HEAD_DIM), lambda bi, i: (bi, i, 0)),
        out_shape=jax.ShapeDtypeStruct((b, s, DSA_HEADS * DSA_HEAD_DIM), MXU_DTYPE),
        scratch_shapes=[pltpu.VMEM((rows, DSA_KV_LATENT), MXU_DTYPE),
                        pltpu.VMEM((Q_BLOCK + KEY_TILE * n_tiles, Q_BLOCK), jnp.int32),
                        pltpu.VMEM((1 + TILE_BLOCKS * n_tiles, Q_BLOCK, Q_BLOCK), jnp.int32),
                        pltpu.VMEM((rows, LANES), jnp.float32),
                        pltpu.VMEM((rows, LANES), jnp.float32),
                        pltpu.VMEM((rows, LANES), jnp.float32),
                        pltpu.VMEM((rows, DSA_KV_LATENT), jnp.float32),
                        pltpu.VMEM((rows, 2 * Q_BLOCK), jnp.float32),
                        pltpu.VMEM((1, Q_BLOCK), jnp.int32)],
        compiler_params=pltpu.CompilerParams(
            dimension_semantics=("arbitrary", "arbitrary"), vmem_limit_bytes=VMEM_LIMIT_BYTES),
        name="dsa_attention",
    )(far, proj3, qi_t, w_t, ki, ct, c_pad, wuk_bd, wuv_bd, bias_d)


def _t5_bucket(dist):
    n = jnp.maximum(dist, 0)
    exact = REL_BUCKETS // 2
    nf = jnp.maximum(n, 1).astype(jnp.float32)
    large = exact + (jnp.log(nf / exact) / math.log(REL_MAX_DIST / exact)
                     * (REL_BUCKETS - exact)).astype(jnp.int32)
    large = jnp.minimum(large, REL_BUCKETS - 1)
    return jnp.where(n < exact, n, large)


def _bias_tables(rel_bias, nc):
    n_pos, n_neg = 1024, NSA_CMP_STRIDE * nc + 1024
    per_dist = rel_bias[_t5_bucket(jnp.arange(n_pos))]
    ext = jnp.concatenate([jnp.broadcast_to(per_dist[:1], (n_neg, per_dist.shape[1])), per_dist], axis=0)
    flipped = ext[::-1].T

    def skew(vec, rows):
        m = vec.shape[-1]
        tiled = jnp.tile(vec, (1,) * (vec.ndim - 1) + (rows,))
        return tiled[..., :rows * (m - 1)].reshape(vec.shape[:-1] + (rows, m - 1))

    def toeplitz(offset, count, heads):
        start = n_pos - 1 - offset - (Q_BLOCK - 1)
        seg = flipped[heads, start:start + count + Q_BLOCK]
        return skew(seg, Q_BLOCK)[..., Q_BLOCK - 1:Q_BLOCK - 1 + count]

    nsa, dsa = slice(0, NSA_HEADS), slice(NSA_HEADS, NSA_HEADS + DSA_HEADS)
    bias_w = toeplitz(NSA_WINDOW, Q_BLOCK + NSA_WINDOW, nsa)
    bias_d = toeplitz(Q_BLOCK, 2 * Q_BLOCK, dsa)
    st, qa = NSA_CMP_STRIDE, Q_BLOCK // NSA_CMP_STRIDE
    base = n_pos - 1 - 225 - (st - 1) - st * (qa - 1)
    seg = flipped[nsa, base:base + st * (nc + qa)].reshape(NSA_HEADS, nc + qa, st)
    per_b = jnp.transpose(seg[..., ::-1], (0, 2, 1))
    bias_c = skew(per_b, qa)[..., qa - 1:qa - 1 + nc]
    bias_c = jnp.transpose(bias_c, (0, 2, 1, 3)).reshape(NSA_HEADS, Q_BLOCK, nc)
    return bias_w, bias_c, bias_d, rel_bias[REL_BUCKETS - 1]


def _overlap_t(s):
    n_blk, nc = s // NSA_SEL_LEN, s // NSA_CMP_STRIDE
    cmp_start = np.arange(nc) * NSA_CMP_STRIDE
    sel_start = np.arange(n_blk) * NSA_SEL_LEN
    ov = (cmp_start[None, :] < sel_start[:, None] + NSA_SEL_LEN) & (cmp_start[None, :] + NSA_CMP_LEN > sel_start[:, None])
    return jnp.asarray(ov.astype(np.float32)).astype(MXU_DTYPE)


def _block_diag_pairs(w_uk, w_uv):
    hp = DSA_HEADS // 2
    dh, lat = DSA_HEAD_DIM, DSA_KV_LATENT
    uk = jnp.transpose(w_uk, (1, 2, 0)).reshape(hp, 2, dh, lat)
    uv = jnp.transpose(w_uv, (1, 0, 2)).reshape(hp, 2, lat, dh)
    z_k, z_v = jnp.zeros((hp, dh, lat), w_uk.dtype), jnp.zeros((hp, lat, dh), w_uv.dtype)
    wuk_bd = jnp.concatenate([jnp.concatenate([uk[:, 0], z_k], axis=2),
                              jnp.concatenate([z_k, uk[:, 1]], axis=2)], axis=1)
    wuv_bd = jnp.concatenate([jnp.concatenate([uv[:, 0], z_v], axis=2),
                              jnp.concatenate([z_v, uv[:, 1]], axis=2)], axis=1)
    return wuk_bd.astype(MXU_DTYPE), wuv_bd.astype(MXU_DTYPE)


def _mixers(proj, b, s, pe_k, pe_v, wk1, wk2, wv1, wv2, g_kv, w_uk, w_uv, tables, ovt):
    bias_w, bias_c, bias_d, far = tables
    p3 = proj.reshape(b, s, N_IN_PAD)
    nq, nblk = s // Q_BLOCK, s // Q_BLOCK
    g, dk = NSA_KV_GROUPS, NSA_HEAD_DIM

    def piece(off, width):
        return p3[:, :, off:off + width]

    def heads_major(a, heads, dim):
        return a.reshape(b, nq, Q_BLOCK, heads, dim).transpose(0, 1, 3, 2, 4).reshape(b, nq, heads * Q_BLOCK, dim)

    def chunked(a):
        return a.reshape(b, s // 16, 16, g, dk).transpose(0, 3, 1, 2, 4).reshape(b, g, s // 16, 16 * dk)

    def keys_t(a, pad_blocks):
        t = a.reshape(b, nblk, Q_BLOCK, a.shape[-1]).transpose(0, 1, 3, 2)
        return jnp.pad(t, ((0, 0), (pad_blocks, 0), (0, 0), (0, 0)))

    def rows_padded(a, pad):
        return jnp.pad(a, ((0, 0), (pad, 0), (0, 0)))

    kct = _compress(chunked(piece(COL_KC, 256)), pe_k, wk1, wk2, transposed=True).reshape(b, g * dk, s // 16)
    vcm = _compress(chunked(piece(COL_VC, 256)), pe_v, wv1, wv2, transposed=False)
    vcm = vcm.transpose(0, 2, 1, 3).reshape(b, s // 16, g * dk)
    o_n = _nsa(p3, heads_major(piece(COL_QN, 1024), NSA_HEADS, dk), kct, vcm,
               keys_t(piece(COL_KS, 256), 1), rows_padded(piece(COL_VS, 256), Q_BLOCK),
               keys_t(piece(COL_KW, 256), NSA_WINDOW // Q_BLOCK), rows_padded(piece(COL_VW, 256), NSA_WINDOW),
               bias_w, bias_c, ovt, far)

    ckv = _ckvnorm(proj, g_kv).reshape(b, s, DSA_KV_LATENT)
    ki = piece(COL_SMALL + 48, IDX_DIM)
    qi_t = piece(COL_QI, 512).reshape(b, nq, Q_BLOCK, IDX_HEADS, IDX_DIM).transpose(0, 1, 4, 3, 2)
    qi_t = qi_t.reshape(b, nq, IDX_DIM, IDX_HEADS * Q_BLOCK)
    w_t = piece(COL_SMALL + 80, IDX_HEADS).reshape(b, nq, Q_BLOCK, IDX_HEADS).transpose(0, 1, 3, 2)
    wuk_bd, wuv_bd = _block_diag_pairs(w_uk, w_uv)
    o_d = _dsa(p3, qi_t, w_t, ki, keys_t(ckv, 1), rows_padded(ckv, Q_BLOCK), wuk_bd, wuv_bd, bias_d, far)
    return o_n.reshape(b * s, -1), o_d.reshape(b * s, -1)


def _permute_w_in(w_in_l):
    d = w_in_l.shape[0]
    sizes = (1024, 256, 256, 256, 256, 256, 256, 48, 1024, 256, 512, 32, 16, 4096)
    offs = np.concatenate([[0], np.cumsum(sizes)])
    (q_n, kc, vc, ks, vs, kw, vw, g_n, q_d, ckv, qi, ki, wi, a_m) = [
        w_in_l[:, int(offs[k]):int(offs[k + 1])] for k in range(len(sizes))]
    small = jnp.concatenate([g_n, ki, wi, jnp.zeros((d, 32), w_in_l.dtype)], axis=1)
    out = jnp.concatenate([a_m, q_n, q_d, kc, vc, ks, vs, kw, vw, qi, ckv, small,
                           jnp.zeros((d, N_IN_PAD - 8576), w_in_l.dtype)], axis=1)
    return out.astype(MXU_DTYPE)


def kernel(x, c, w_ada, b_ada, g_norm, w_ffn_in, w_ffn_out, w_in, nsa_pe_k, nsa_pe_v, nsa_cmp_k1, nsa_cmp_k2,
           nsa_cmp_v1, nsa_cmp_v2, dsa_g_kv, dsa_w_uk, dsa_w_uv, w_up_nsa, w_up_dsa, w_out, rel_bias, g_final):
    B, S, D = x.shape
    N = B * S
    mod = _adaln(c, w_ada, b_ada)[:, :B].reshape(DEPTH, B, N_SUBLAYERS, 3, 1, D)
    tables = _bias_tables(rel_bias, S // NSA_CMP_STRIDE)
    ovt = _overlap_t(S)
    x2d = x.reshape(N, D)
    for l in range(DEPTH):
        m0, m1, m2 = mod[l, :, 0], mod[l, :, 1], mod[l, :, 2]
        x2d = _ffn(x2d, m0[:, 0], m0[:, 1], m0[:, 2], g_norm[l, 0],
                   w_ffn_in[l, 0].astype(MXU_DTYPE), w_ffn_out[l, 0].astype(MXU_DTYPE), g_final,
                   seq=S, final_norm=False)
        proj = _inproj(x2d, m1[:, 0], m1[:, 1], g_norm[l, 1], _permute_w_in(w_in[l]), seq=S)
        o_n, o_d = _mixers(proj, B, S, nsa_pe_k[l], nsa_pe_v[l], nsa_cmp_k1[l], nsa_cmp_k2[l], nsa_cmp_v1[l],
                           nsa_cmp_v2[l], dsa_g_kv[l], dsa_w_uk[l], dsa_w_uv[l], tables, ovt)
        x2d = _merge(x2d, m1[:, 2], o_n, o_d, proj, w_up_nsa[l].astype(MXU_DTYPE),
                     w_up_dsa[l].astype(MXU_DTYPE), w_out[l].astype(MXU_DTYPE), seq=S)
        x2d = _ffn(x2d, m2[:, 0], m2[:, 1], m2[:, 2], g_norm[l, 2],
                   w_ffn_in[l, 1].astype(MXU_DTYPE), w_ffn_out[l, 1].astype(MXU_DTYPE), g_final,
                   seq=S, final_norm=(l == DEPTH - 1))
    return x2d.reshape(B, S, D)
```

```python
import functools
import math

import jax
import jax.numpy as jnp
import numpy as np
from jax import lax
from jax.experimental import pallas as pl
from jax.experimental.pallas import tpu as pltpu

D_MODEL = 2048
DEPTH = 2
NSA_HEADS = 16
NSA_KV_GROUPS = 4
NSA_HEAD_DIM = 64
NSA_CMP_LEN = 32
NSA_CMP_STRIDE = 16
NSA_CMP_HIDDEN = 256
NSA_SEL_LEN = 64
NSA_SEL_BLOCKS = 16
NSA_WINDOW = 512
DSA_HEADS = 16
DSA_HEAD_DIM = 64
DSA_KV_LATENT = 256
IDX_HEADS = 16
IDX_DIM = 32
DSA_TOPK_MAX = 256
INDEX_SCALE = (IDX_HEADS * IDX_DIM) ** -0.5
D_FF = 5632
REL_BUCKETS = 32
REL_MAX_DIST = 128
Q_BLOCK = 128
N_SUBLAYERS = 3
RMS_EPS = 1e-6
NEG_INF = -1e30
FORCE_BONUS = 1e4
NSA_KV_WIDTH = NSA_KV_GROUPS * NSA_HEAD_DIM

MXU_DTYPE = jnp.bfloat16
VMEM_LIMIT_BYTES = 56 * 1024 * 1024
LANES = 128
KEY_TILE = 512
TILE_BLOCKS = KEY_TILE // Q_BLOCK
TILE_SEL_BLOCKS = KEY_TILE // NSA_SEL_LEN
SEL_SHIFT = NSA_SEL_LEN.bit_length() - 1

COL_AM = 0
COL_QN = 4096
COL_QD = 5120
COL_KC, COL_VC, COL_KS, COL_VS, COL_KW, COL_VW = 6144, 6400, 6656, 6912, 7168, 7424
COL_QI = 7680
COL_CKV = 8192
COL_SMALL = 8448
N_IN_PAD = 8704


def _dot(a, b):
    return jnp.dot(a, b, preferred_element_type=jnp.float32)


def _rms_scale(x):
    return x * lax.rsqrt(jnp.mean(x * x, axis=-1, keepdims=True) + RMS_EPS)


def _modulated_norm(x, g, shift, scale):
    return (_rms_scale(x) * g) * (1.0 + scale) + shift


def _split_hi_lo(x):
    hi = x.astype(MXU_DTYPE)
    return hi, (x - hi.astype(jnp.float32)).astype(MXU_DTYPE)


def _adaln_kernel(c_ref, w_ref, b_ref, o_ref):
    c = c_ref[...]
    c_hi, c_lo = _split_hi_lo(c * jax.nn.sigmoid(c))
    w_hi, w_lo = _split_hi_lo(w_ref[0])
    o_ref[0] = _dot(c_hi, w_hi) + _dot(c_hi, w_lo) + _dot(c_lo, w_hi) + b_ref[0]


def _adaln(c, w_ada, b_ada):
    depth, d, n = w_ada.shape
    rows = 16
    c = jnp.pad(c, ((0, rows - c.shape[0]), (0, 0)))
    b = rows
    tn = 1024
    return pl.pallas_call(
        _adaln_kernel,
        grid=(depth, n // tn),
        in_specs=[
            pl.BlockSpec((b, d), lambda l, j: (0, 0)),
            pl.BlockSpec((1, d, tn), lambda l, j: (l, 0, j)),
            pl.BlockSpec((1, 1, tn), lambda l, j: (l, 0, j)),
        ],
        out_specs=pl.BlockSpec((1, b, tn), lambda l, j: (l, 0, j)),
        out_shape=jax.ShapeDtypeStruct((depth, b, n), jnp.float32),
        compiler_params=pltpu.CompilerParams(
            dimension_semantics=("arbitrary", "arbitrary"), vmem_limit_bytes=VMEM_LIMIT_BYTES),
        name="adaln",
    )(c, w_ada, b_ada.reshape(depth, 1, n))


def _ffn_kernel(x_ref, shift_ref, scale_ref, gate_ref, g_ref, wg_ref, wu_ref, wo_ref, gfin_ref, o_ref,
                h_scr, acc_scr, *, final_norm):
    j = pl.program_id(1)

    @pl.when(j == 0)
    def _():
        h = _modulated_norm(x_ref[...], g_ref[...], shift_ref[0], scale_ref[0])
        h_scr[...] = h.astype(h_scr.dtype)
        acc_scr[...] = jnp.zeros_like(acc_scr)

    h = h_scr[...]
    g = _dot(h, wg_ref[...])
    u = _dot(h, wu_ref[...])
    a = (g * jax.nn.sigmoid(g)) * u
    acc_scr[...] += _dot(a.astype(wo_ref.dtype), wo_ref[...])

    @pl.when(j == pl.num_programs(1) - 1)
    def _():
        y = x_ref[...] + (0.5 * gate_ref[0]) * acc_scr[...]
        if final_norm:
            y = _rms_scale(y) * gfin_ref[...]
        o_ref[...] = y


def _ffn(x2d, shift, scale, gate, g, w_i, w_o, g_final, *, seq, final_norm):
    n, d = x2d.shape
    tm, tf = 512, 512
    nf = D_FF // tf
    per_b = seq // tm
    mod_spec = pl.BlockSpec((1, 1, d), lambda i, j: (i // per_b, 0, 0))
    return pl.pallas_call(
        functools.partial(_ffn_kernel, final_norm=final_norm),
        grid=(n // tm, nf),
        in_specs=[
            pl.BlockSpec((tm, d), lambda i, j: (i, 0)),
            mod_spec, mod_spec, mod_spec,
            pl.BlockSpec((1, d), lambda i, j: (0, 0)),
            pl.BlockSpec((d, tf), lambda i, j: (0, j)),
            pl.BlockSpec((d, tf), lambda i, j: (0, j + nf)),
            pl.BlockSpec((tf, d), lambda i, j: (j, 0)),
            pl.BlockSpec((1, d), lambda i, j: (0, 0)),
        ],
        out_specs=pl.BlockSpec((tm, d), lambda i, j: (i, 0)),
        out_shape=jax.ShapeDtypeStruct((n, d), jnp.float32),
        scratch_shapes=[pltpu.VMEM((tm, d), MXU_DTYPE), pltpu.VMEM((tm, d), jnp.float32)],
        compiler_params=pltpu.CompilerParams(
            dimension_semantics=("arbitrary", "arbitrary"), vmem_limit_bytes=VMEM_LIMIT_BYTES),
        name="ffn",
    )(x2d, shift, scale, gate, g.reshape(1, d), w_i, w_i, w_o, g_final.reshape(1, d))


def _inproj_kernel(x_ref, shift_ref, scale_ref, g_ref, w_ref, o_ref, h_scr):
    @pl.when(pl.program_id(1) == 0)
    def _():
        h = _modulated_norm(x_ref[...], g_ref[...], shift_ref[0], scale_ref[0])
        h_scr[...] = h.astype(h_scr.dtype)

    o_ref[...] = _dot(h_scr[...], w_ref[...]).astype(o_ref.dtype)


def _inproj(x2d, shift, scale, g, w, *, seq):
    n, d = x2d.shape
    tm, tn = 512, N_IN_PAD // 4
    per_b = seq // tm
    mod_spec = pl.BlockSpec((1, 1, d), lambda i, j: (i // per_b, 0, 0))
    return pl.pallas_call(
        _inproj_kernel,
        grid=(n // tm, N_IN_PAD // tn),
        in_specs=[
            pl.BlockSpec((tm, d), lambda i, j: (i, 0)),
            mod_spec, mod_spec,
            pl.BlockSpec((1, d), lambda i, j: (0, 0)),
            pl.BlockSpec((d, tn), lambda i, j: (0, j)),
        ],
        out_specs=pl.BlockSpec((tm, tn), lambda i, j: (i, j)),
        out_shape=jax.ShapeDtypeStruct((n, N_IN_PAD), MXU_DTYPE),
        scratch_shapes=[pltpu.VMEM((tm, d), MXU_DTYPE)],
        compiler_params=pltpu.CompilerParams(
            dimension_semantics=("arbitrary", "arbitrary"), vmem_limit_bytes=VMEM_LIMIT_BYTES),
        name="inproj",
    )(x2d, shift, scale, g.reshape(1, d), w)


def _merge_kernel(x_ref, gate_ref, on_ref, od_ref, a0_ref, a1_ref, wun_ref, wud_ref, wout_ref, o_ref):
    yn = _dot(on_ref[...], wun_ref[...])
    yd = _dot(od_ref[...], wud_ref[...])
    a0 = jax.nn.sigmoid(a0_ref[...].astype(jnp.float32))
    a1 = jax.nn.sigmoid(a1_ref[...].astype(jnp.float32))
    y = a0 * yn + a1 * yd
    o_ref[...] = x_ref[...] + gate_ref[0] * _dot(y.astype(wout_ref.dtype), wout_ref[...])


def _merge(x2d, gate, o_n, o_d, proj, w_up_nsa, w_up_dsa, w_out, *, seq):
    n, d = x2d.shape
    tm = 512
    per_b = seq // tm
    hw = o_n.shape[1]
    const = dict(pipeline_mode=pl.Buffered(1))
    return pl.pallas_call(
        _merge_kernel,
        grid=(n // tm,),
        in_specs=[
            pl.BlockSpec((tm, d), lambda i: (i, 0)),
            pl.BlockSpec((1, 1, d), lambda i: (i // per_b, 0, 0)),
            pl.BlockSpec((tm, hw), lambda i: (i, 0)),
            pl.BlockSpec((tm, hw), lambda i: (i, 0)),
            pl.BlockSpec((tm, d), lambda i: (i, COL_AM // d)),
            pl.BlockSpec((tm, d), lambda i: (i, COL_AM // d + 1)),
            pl.BlockSpec((hw, d), lambda i: (0, 0), **const),
            pl.BlockSpec((hw, d), lambda i: (0, 0), **const),
            pl.BlockSpec((d, d), lambda i: (0, 0), **const),
        ],
        out_specs=pl.BlockSpec((tm, d), lambda i: (i, 0)),
        out_shape=jax.ShapeDtypeStruct((n, d), jnp.float32),
        compiler_params=pltpu.CompilerParams(
            dimension_semantics=("arbitrary",), vmem_limit_bytes=VMEM_LIMIT_BYTES),
        name="merge",
    )(x2d, gate, o_n, o_d, proj, proj, w_up_nsa, w_up_dsa, w_out)


def _dot_nt(a, b):
    return lax.dot_general(a, b, (((1,), (1,)), ((), ())), preferred_element_type=jnp.float32)


def _iota(shape, dim):
    return lax.broadcasted_iota(jnp.int32, shape, dim)


def _softmax_rows(s, mask):
    s = jnp.where(mask, s, NEG_INF)
    m = jnp.max(s, axis=1, keepdims=True)
    p = jnp.where(mask, jnp.exp(s - m), 0.0)
    return p * (1.0 / jnp.maximum(jnp.sum(p, axis=1, keepdims=True), 1e-30))


def _lane_tiles(x):
    return [x[:, u * LANES:(u + 1) * LANES] for u in range(x.shape[1] // LANES)]


def _max_pass(s_parts, mask, rows0, mx_scr):
    qb = s_parts[0].shape[0]
    for r, s in enumerate(s_parts):
        tiles = _lane_tiles(jnp.where(mask, s, NEG_INF))
        red = functools.reduce(jnp.maximum, tiles)
        sl = slice(rows0 + r * qb, rows0 + (r + 1) * qb)
        mx_scr[sl, :] = jnp.maximum(mx_scr[sl, :], red)


def _exp_pass(s_parts, mask, shifts, v_tile, rows0, ls_scr, acc_scr):
    qb = s_parts[0].shape[0]
    ps = []
    for r, (s, shift) in enumerate(zip(s_parts, shifts)):
        if mask is not None:
            s = jnp.where(mask, s, NEG_INF)
        ptiles = [jnp.exp(t - shift) for t in _lane_tiles(s)]
        sl = slice(rows0 + r * qb, rows0 + (r + 1) * qb)
        ls_scr[sl, :] = ls_scr[sl, :] + functools.reduce(jnp.add, ptiles)
        ps.append(jnp.concatenate(ptiles, axis=1))
    p = jnp.concatenate(ps, axis=0).astype(v_tile.dtype)
    rows = slice(rows0, rows0 + len(s_parts) * qb)
    acc_scr[rows, :] = acc_scr[rows, :] + _dot(p, v_tile)


def _compress_kernel(x_ref, pe_ref, w1_ref, w2_ref, o_ref, *, transposed):
    x = x_ref[0, 0]
    half = x.shape[1]
    first = _dot(x, w1_ref[0:half, :])
    second = _dot(x, w1_ref[half:2 * half, :])
    pe_term = _dot(pe_ref[...], w1_ref[...])[0:1]
    hid = first + pltpu.roll(second, x.shape[0] - 1, 0) + pe_term
    hid = (hid * jax.nn.sigmoid(hid)).astype(w2_ref.dtype)
    if transposed:
        o_ref[0, 0] = _dot_nt(w2_ref[...], hid).astype(o_ref.dtype)
    else:
        o_ref[0, 0] = _dot(hid, w2_ref[...]).astype(o_ref.dtype)


def _compress(x_chunks, pe, w1, w2, *, transposed):
    b, g, nc, width = x_chunks.shape
    dk = NSA_HEAD_DIM
    pe8 = jnp.zeros((8, 2 * width), MXU_DTYPE).at[0].set(pe.reshape(-1).astype(MXU_DTYPE))
    w2m = (w2.T if transposed else w2).astype(MXU_DTYPE)
    oshape = (b, g, dk, nc) if transposed else (b, g, nc, dk)
    return pl.pallas_call(
        functools.partial(_compress_kernel, transposed=transposed),
        grid=(b, g),
        in_specs=[
            pl.BlockSpec((1, 1, nc, width), lambda i, j: (i, j, 0, 0)),
            pl.BlockSpec((8, 2 * width), lambda i, j: (0, 0)),
            pl.BlockSpec((2 * width, NSA_CMP_HIDDEN), lambda i, j: (0, 0)),
            pl.BlockSpec(w2m.shape, lambda i, j: (0, 0)),
        ],
        out_specs=pl.BlockSpec((1, 1) + oshape[2:], lambda i, j: (i, j, 0, 0)),
        out_shape=jax.ShapeDtypeStruct(oshape, MXU_DTYPE),
        compiler_params=pltpu.CompilerParams(dimension_semantics=("arbitrary", "arbitrary")),
        name="nsa_compress",
    )(x_chunks, pe8, w1.astype(MXU_DTYPE), w2m)


def _ckvnorm_kernel(x_ref, g_ref, o_ref):
    x = x_ref[...].astype(jnp.float32)
    o_ref[...] = (_rms_scale(x) * g_ref[...]).astype(o_ref.dtype)


def _ckvnorm(proj, g_kv):
    n = proj.shape[0]
    tm = 2048
    return pl.pallas_call(
        _ckvnorm_kernel,
        grid=(n // tm,),
        in_specs=[pl.BlockSpec((tm, DSA_KV_LATENT), lambda i: (i, COL_CKV // DSA_KV_LATENT)),
                  pl.BlockSpec((1, DSA_KV_LATENT), lambda i: (0, 0))],
        out_specs=pl.BlockSpec((tm, DSA_KV_LATENT), lambda i: (i, 0)),
        out_shape=jax.ShapeDtypeStruct((n, DSA_KV_LATENT), MXU_DTYPE),
        compiler_params=pltpu.CompilerParams(dimension_semantics=("arbitrary",)),
        name="ckv_norm",
    )(proj, g_kv.reshape(1, -1).astype(jnp.float32))


def _nsa_kernel(far_ref, q_ref, small_ref, kct_ref, vc_ref, kst_ref, vs_ref, kwt_ref, vw_ref, bw_ref, bc_ref,
                ovt_ref, o_ref, qs_scr, sel_scr, chosen_scr, mxf_scr, mb_scr, ls_scr, acc_scr, sn_scr, oc_scr,
                ow_scr):
    i = pl.program_id(1)
    t0 = i * Q_BLOCK
    qb, dk, hg = Q_BLOCK, NSA_HEAD_DIM, NSA_HEADS // NSA_KV_GROUPS
    grows = hg * qb
    nc = kct_ref.shape[2]
    nb = ovt_ref.shape[0]
    wk = qb + NSA_WINDOW
    f32 = jnp.float32

    gates = jax.nn.sigmoid(small_ref[0].astype(f32))
    n_io = _iota((qb, nc), 1)
    tc_io = t0 + _iota((qb, nc), 0)
    mask_c = NSA_CMP_STRIDE * n_io + (NSA_CMP_LEN - 1) <= tc_io
    near_lo = (qb // NSA_CMP_STRIDE) * i - 16
    near_c = n_io >= near_lo
    shift_c = jnp.where(near_lo >= 0, near_lo, near_lo + nc)
    j_io = _iota((nb, qb), 0)
    tj_io = t0 + _iota((nb, qb), 1)
    cur = jnp.right_shift(tj_io, SEL_SHIFT)
    forced = (j_io == 0) | (j_io == cur) | (j_io == cur - 1)
    visible = j_io * NSA_SEL_LEN <= tj_io
    eye = jnp.where(_iota((qb, qb), 0) == _iota((qb, qb), 1), 1.0, 0.0).astype(MXU_DTYPE)
    kw_io = _iota((qb, wk), 1)
    off_w = _iota((qb, wk), 0) - kw_io + NSA_WINDOW
    mask_w = (off_w >= 0) & (off_w < NSA_WINDOW) & (t0 - NSA_WINDOW + kw_io >= 0)
    kn_io = _iota((qb, 2 * qb), 1)
    pos_n = t0 - qb + kn_io
    causal_n = (pos_n <= t0 + _iota((qb, 2 * qb), 0)) & (pos_n >= 0)
    n_far = (i - 1 + TILE_BLOCKS - 1) // TILE_BLOCKS
    far_blocks = (qb // NSA_SEL_LEN) * (i - 1)
    jsel_io = _iota((qb, nb), 1)
    qs_scr[...] = q_ref[0, 0] * (dk ** -0.5)

    def value_slab(g):
        return slice((g // 2) * LANES, (g // 2) * LANES + LANES)

    for g in range(NSA_KV_GROUPS):
        rows = pl.ds(g * grows, grows)
        gsl = slice(g * dk, (g + 1) * dk)
        slab = value_slab(g)
        qg = qs_scr[g * grows:(g + 1) * grows, :]

        s = _dot(qg, kct_ref[0, gsl, :])
        probs = []
        for r in range(hg):
            h = g * hg + r
            bias = jnp.where(near_c, pltpu.roll(bc_ref[h], shift_c, 1), far_ref[h])
            probs.append(_softmax_rows(s[r * qb:(r + 1) * qb] + bias, mask_c))
        pstack = jnp.concatenate(probs, axis=0)
        oc_scr[rows, :] = _dot(pstack.astype(vc_ref.dtype), vc_ref[0, :, slab])
        pg = probs[0] + probs[1] + probs[2] + probs[3]
        hi = pg.astype(MXU_DTYPE)
        r1 = pg - hi.astype(f32)
        mid = r1.astype(MXU_DTYPE)
        lo = (r1 - mid.astype(f32)).astype(MXU_DTYPE)
        ovt = ovt_ref[...]
        imp = _dot_nt(ovt, hi) + _dot_nt(ovt, mid) + _dot_nt(ovt, lo)
        imp = jnp.where(visible, imp + jnp.where(forced, FORCE_BONUS, 0.0), NEG_INF)
        ranks = [jnp.zeros((nb, qb), f32) for _ in range(4)]
        for jp in range(nb):
            row = imp[jp:jp + 1, :]
            beats = jnp.where(row > imp, 1.0, jnp.where(row == imp, jnp.where(jp < j_io, 1.0, 0.0), 0.0))
            ranks[jp % 4] = ranks[jp % 4] + beats
        rank = (ranks[0] + ranks[1]) + (ranks[2] + ranks[3])
        sel_t = jnp.where(rank < NSA_SEL_BLOCKS, 1.0, 0.0).astype(MXU_DTYPE)
        sel = _dot_nt(eye, sel_t)
        sel_scr[2 * g] = sel
        sel_scr[2 * g + 1] = jnp.where(jsel_io < far_blocks, sel, 0.0)

        kwt = jnp.concatenate([kwt_ref[0, i + u, gsl, :] for u in range(wk // qb)], axis=1)
        s = _dot(qg, kwt)
        probs, sums = [], []
        for r in range(hg):
            sh = jnp.where(mask_w, s[r * qb:(r + 1) * qb] + bw_ref[g * hg + r], NEG_INF)
            p = jnp.exp(sh - jnp.max(sh, axis=1, keepdims=True))
            probs.append(p)
            sums.append(jnp.sum(p, axis=1, keepdims=True))
        pstack = jnp.concatenate(probs, axis=0)
        ow = _dot(pstack.astype(vw_ref.dtype), vw_ref[0, pl.ds(pl.multiple_of(t0, qb), wk), slab])
        for r in range(hg):
            ow_scr[g * grows + r * qb:g * grows + (r + 1) * qb, :] = ow[r * qb:(r + 1) * qb] * (1.0 / sums[r])

    def far_tile(jt, g, first_pass):
        gsl = slice(g * dk, (g + 1) * dk)
        kt = jnp.concatenate([kst_ref[0, 1 + TILE_BLOCKS * jt + u, gsl, :] for u in range(TILE_BLOCKS)], axis=1)
        if first_pass:
            expand = jnp.where(_iota((nb, KEY_TILE), 0)
                               == TILE_SEL_BLOCKS * jt + jnp.right_shift(_iota((nb, KEY_TILE), 1), SEL_SHIFT),
                               1.0, 0.0).astype(MXU_DTYPE)
            chosen = _dot(sel_scr[2 * g + 1].astype(MXU_DTYPE), expand)
            chosen_scr[NSA_KV_GROUPS * jt + g] = chosen
        else:
            chosen = chosen_scr[NSA_KV_GROUPS * jt + g]
        sc = _dot(qs_scr[g * grows:(g + 1) * grows, :], kt)
        return [sc[r * qb:(r + 1) * qb] for r in range(hg)], chosen > 0.5

    mxf_scr[...] = jnp.full(mxf_scr.shape, NEG_INF, f32)
    ls_scr[...] = jnp.zeros(ls_scr.shape, f32)
    acc_scr[...] = jnp.zeros(acc_scr.shape, f32)

    def far_max_body(jt):
        for g in range(NSA_KV_GROUPS):
            parts, mask = far_tile(jt, g, True)
            _max_pass(parts, mask, g * grows, mxf_scr)

    _for_each_tile_paired(n_far, far_max_body)

    expand_n = jnp.where(_iota((nb, 2 * qb), 0)
                         == far_blocks + jnp.right_shift(_iota((nb, 2 * qb), 1), SEL_SHIFT),
                         1.0, 0.0).astype(MXU_DTYPE)
    for g in range(NSA_KV_GROUPS):
        gsl = slice(g * dk, (g + 1) * dk)
        kt = jnp.concatenate([kst_ref[0, i + u, gsl, :] for u in range(2)], axis=1)
        mask = (_dot(sel_scr[2 * g].astype(MXU_DTYPE), expand_n) > 0.5) & causal_n
        sc = _dot(qs_scr[g * grows:(g + 1) * grows, :], kt)
        for r in range(hg):
            h = g * hg + r
            hr = slice(g * grows + r * qb, g * grows + (r + 1) * qb)
            sn = jnp.where(mask, sc[r * qb:(r + 1) * qb] + bw_ref[h][:, NSA_WINDOW - qb:], NEG_INF)
            sn_scr[hr, :] = sn
            m_far = jnp.max(mxf_scr[hr, :], axis=1, keepdims=True) + far_ref[h]
            m_row = jnp.maximum(m_far, jnp.max(sn, axis=1, keepdims=True))
            mb_scr[hr, :] = jnp.broadcast_to(m_row, (qb, LANES))

    def far_exp_body(jt):
        vt = vs_ref[0, pl.ds(pl.multiple_of(qb + KEY_TILE * jt, qb), KEY_TILE), :]
        for g in range(NSA_KV_GROUPS):
            parts, mask = far_tile(jt, g, False)
            shifts = [mb_scr[g * grows + r * qb:g * grows + (r + 1) * qb, :] - far_ref[g * hg + r]
                      for r in range(hg)]
            _exp_pass(parts, mask, shifts, vt[:, value_slab(g)], g * grows, ls_scr, acc_scr)

    _for_each_tile_paired(n_far, far_exp_body)

    vt_n = vs_ref[0, pl.ds(pl.multiple_of(t0, qb), 2 * qb), :]
    for g in range(NSA_KV_GROUPS):
        hrs = [slice(g * grows + r * qb, g * grows + (r + 1) * qb) for r in range(hg)]
        _exp_pass([sn_scr[hr, :] for hr in hrs], None, [mb_scr[hr, :] for hr in hrs], vt_n[:, value_slab(g)],
                  g * grows, ls_scr, acc_scr)

    for g in range(NSA_KV_GROUPS):
        placed = []
        for r in range(hg):
            h = g * hg + r
            hr = slice(g * grows + r * qb, g * grows + (r + 1) * qb)
            o_s = acc_scr[hr, :] * (1.0 / jnp.sum(ls_scr[hr, :], axis=1, keepdims=True))
            o_h = (gates[:, 3 * h:3 * h + 1] * oc_scr[hr, :] + gates[:, 3 * h + 1:3 * h + 2] * o_s
                   + gates[:, 3 * h + 2:3 * h + 3] * ow_scr[hr, :])
            if g % 2 != r % 2:
                o_h = pltpu.roll(o_h, dk, 1)
            placed.append(o_h)
        low_half = _iota((qb, LANES), 1) < dk
        for pr in range(hg // 2):
            pair = jnp.where(low_half, placed[2 * pr], placed[2 * pr + 1])
            col = (g * hg // 2 + pr) * LANES
            o_ref[0, :, col:col + LANES] = pair.astype(o_ref.dtype)


def _nsa(proj3, q_r, kct, vc, kst, vs, kwt, vw, bias_w, bias_c, ovt, far):
    b, s, _ = proj3.shape
    nq = s // Q_BLOCK
    rows = NSA_HEADS * Q_BLOCK
    vdim = NSA_KV_WIDTH
    const = dict(pipeline_mode=pl.Buffered(1))

    def whole(a):
        nd = a.ndim
        return pl.BlockSpec((1,) + a.shape[1:], lambda bi, i: (bi,) + (0,) * (nd - 1))

    return pl.pallas_call(
        _nsa_kernel,
        grid=(b, nq),
        in_specs=[
            pl.BlockSpec(memory_space=pltpu.SMEM),
            pl.BlockSpec((1, 1, rows, NSA_HEAD_DIM), lambda bi, i: (bi, i, 0, 0)),
            pl.BlockSpec((1, Q_BLOCK, 128), lambda bi, i: (bi, i, COL_SMALL // 128)),
            whole(kct), whole(vc), whole(kst), whole(vs), whole(kwt), whole(vw),
            pl.BlockSpec(bias_w.shape, lambda bi, i: (0, 0, 0), **const),
            pl.BlockSpec(bias_c.shape, lambda bi, i: (0, 0, 0), **const),
            pl.BlockSpec(ovt.shape, lambda bi, i: (0, 0), **const),
        ],
        out_specs=pl.BlockSpec((1, Q_BLOCK, NSA_HEADS * NSA_HEAD_DIM), lambda bi, i: (bi, i, 0)),
        out_shape=jax.ShapeDtypeStruct((b, s, NSA_HEADS * NSA_HEAD_DIM), MXU_DTYPE),
        scratch_shapes=[pltpu.VMEM((rows, NSA_HEAD_DIM), MXU_DTYPE),
                        pltpu.VMEM((2 * NSA_KV_GROUPS, Q_BLOCK, ovt.shape[0]), jnp.float32),
                        pltpu.VMEM((NSA_KV_GROUPS * (-(-nq // TILE_BLOCKS)), Q_BLOCK, KEY_TILE),
                                   jnp.float32),
                        pltpu.VMEM((rows, LANES), jnp.float32),
                        pltpu.VMEM((rows, LANES), jnp.float32),
                        pltpu.VMEM((rows, LANES), jnp.float32),
                        pltpu.VMEM((rows, LANES), jnp.float32),
                        pltpu.VMEM((rows, 2 * Q_BLOCK), jnp.float32),
                        pltpu.VMEM((rows, LANES), jnp.float32),
                        pltpu.VMEM((rows, LANES), jnp.float32)],
        compiler_params=pltpu.CompilerParams(
            dimension_semantics=("arbitrary", "arbitrary"), vmem_limit_bytes=VMEM_LIMIT_BYTES),
        name="nsa_attention",
    )(far, q_r, proj3, kct, vc, kst, vs, kwt, vw, bias_w, bias_c, ovt)


INT_MIN = -2 ** 31
DSA_ATT_HEADS_PER_CHUNK = 4


def _for_each_tile_paired(n, body):
    def pair(p, carry):
        body(2 * p)
        body(2 * p + 1)
        return carry

    lax.fori_loop(0, n // 2, pair, 0)

    @pl.when(n % 2 == 1)
    def _():
        body(n - 1)


def _sortable_key(x):
    bits = pltpu.bitcast(x, jnp.int32)
    return bits ^ (jnp.right_shift(bits, 31) & 0x7FFFFFFF)


def _dsa_kernel(far_ref, qd_ref, qit_ref, wt_ref, ki_ref, ct_ref, c_ref, wuk_ref, wuv_ref, bd_ref, o_ref,
                qlat_scr, keyt_scr, key_scr, mxf_scr, mb_scr, ls_scr, acc_scr, sn_scr, tie_scr):
    i = pl.program_id(1)
    t0 = i * Q_BLOCK
    qb = Q_BLOCK
    f32 = jnp.float32
    k_top = float(DSA_TOPK_MAX)
    chunk = 4 * qb
    n_chunks = DSA_HEADS // 4

    for pr in range(DSA_HEADS // 2):
        r = _dot(qd_ref[0, :, pr * 128:(pr + 1) * 128], wuk_ref[pr]) * (DSA_HEAD_DIM ** -0.5)
        qlat_scr[2 * pr * qb:(2 * pr + 1) * qb, :] = r[:, :DSA_KV_LATENT].astype(qlat_scr.dtype)
        qlat_scr[(2 * pr + 1) * qb:(2 * pr + 2) * qb, :] = r[:, DSA_KV_LATENT:].astype(qlat_scr.dtype)

    n_tiles = i // TILE_BLOCKS + 1
    w_t = wt_ref[0, 0].astype(f32) * INDEX_SCALE
    tq_io = t0 + _iota((KEY_TILE, qb), 1)
    keyt_scr[0:qb, :] = jnp.full((qb, qb), INT_MIN, jnp.int32)

    def tile_rows(jt):
        return pl.ds(pl.multiple_of(qb + KEY_TILE * jt, qb), KEY_TILE)

    def index_body(jt):
        kt = ki_ref[0, pl.ds(pl.multiple_of(KEY_TILE * jt, KEY_TILE), KEY_TILE), :]
        sc = jnp.zeros((KEY_TILE, qb), f32)
        for hc in range(n_chunks):
            lg = _dot(kt, qit_ref[0, 0, :, hc * chunk:(hc + 1) * chunk])
            for r in range(4):
                h = hc * 4 + r
                sc = sc + jnp.maximum(lg[:, r * qb:(r + 1) * qb], 0.0) * w_t[h:h + 1, :]
        sc = jnp.where(KEY_TILE * jt + _iota((KEY_TILE, qb), 0) <= tq_io, sc, NEG_INF)
        keyt_scr[tile_rows(jt), :] = _sortable_key(sc)

    _for_each_tile_paired(n_tiles, index_body)

    def count(pred):
        def body(jt, c):
            x = jnp.where(pred(keyt_scr[tile_rows(jt), :], KEY_TILE * jt + _iota((KEY_TILE, qb), 0)), 1.0, 0.0)
            parts = [x[8 * v:8 * v + 8] for v in range(KEY_TILE // 8)]
            while len(parts) > 1:
                parts = [a + b for a, b in zip(parts[0::2], parts[1::2])]
            return c + parts[0]
        c = lax.fori_loop(0, n_tiles // 2, lambda p, c: body(2 * p + 1, body(2 * p, c)), jnp.zeros((8, qb), f32))
        c = lax.fori_loop(2 * (n_tiles // 2), n_tiles, body, c)
        return jnp.sum(c, axis=0, keepdims=True)

    thr = jnp.where(count(lambda kk, pos: kk >= 0) >= k_top, 0, INT_MIN).astype(jnp.int32)
    thr = thr + jnp.zeros((1, qb), jnp.int32)

    def bit_body(bit, thr):
        cand = thr | jnp.left_shift(jnp.int32(1), 30 - bit)
        return jnp.where(count(lambda kk, pos: kk >= cand) >= k_top, cand, thr)

    thr = lax.fori_loop(0, 31, bit_body, thr)
    need = k_top - count(lambda kk, pos: kk > thr)
    n_ge = count(lambda kk, pos: kk >= thr)
    tie_scr[...] = jnp.full((1, qb), 2 ** 30, jnp.int32)
    has_excess = jnp.max(jnp.where(n_ge > k_top, 1.0, 0.0)) > 0.5

    @pl.when(has_excess)
    def _():
        def pos_body(bit, last):
            cand = last | jnp.left_shift(jnp.int32(1), 12 - bit)
            n_before = count(lambda kk, pos: (kk == thr) & (pos < cand))
            return jnp.where(n_before < need, cand, last)
        tie_scr[...] = lax.fori_loop(0, 13, pos_body, jnp.zeros((1, qb), jnp.int32))

    last_tie = tie_scr[...]
    eye = jnp.where(_iota((qb, qb), 0) == _iota((qb, qb), 1), 1.0, 0.0).astype(MXU_DTYPE)

    def selected_rows(kk, pos, extra=None):
        chosen = (kk > thr) | ((kk == thr) & (pos <= last_tie))
        if extra is not None:
            chosen = chosen & extra
        return _dot_nt(eye, jnp.where(chosen, 1.0, 0.0).astype(MXU_DTYPE)) > 0.5

    pos_n = t0 - qb + _iota((2 * qb, qb), 0)
    causal_n = (pos_n <= t0 + _iota((2 * qb, qb), 1)) & (pos_n >= 0)
    mask_n = selected_rows(keyt_scr[pl.ds(pl.multiple_of(t0, qb), 2 * qb), :], pos_n, causal_n)

    def mask_body(jt):
        mt = selected_rows(keyt_scr[tile_rows(jt), :], KEY_TILE * jt + _iota((KEY_TILE, qb), 0))
        for u in range(TILE_BLOCKS):
            bi = TILE_BLOCKS * jt + u
            key_scr[1 + bi] = jnp.where(mt[:, u * qb:(u + 1) * qb], jnp.where(bi < i - 1, 1, 0), 0)

    _for_each_tile_paired(n_tiles, mask_body)

    mxf_scr[...] = jnp.full(mxf_scr.shape, NEG_INF, f32)
    ls_scr[...] = jnp.zeros(ls_scr.shape, f32)
    acc_scr[...] = jnp.zeros(acc_scr.shape, f32)
    n_far = (i - 1 + TILE_BLOCKS - 1) // TILE_BLOCKS

    hpc = DSA_ATT_HEADS_PER_CHUNK
    arows = hpc * qb
    n_att = DSA_HEADS // hpc

    def far_tile(jt, hc):
        ct = jnp.concatenate([ct_ref[0, 1 + TILE_BLOCKS * jt + u] for u in range(TILE_BLOCKS)], axis=1)
        sc = _dot(qlat_scr[hc * arows:(hc + 1) * arows, :], ct)
        return [sc[r * qb:(r + 1) * qb] for r in range(hpc)]

    def far_mask(jt):
        return jnp.concatenate([key_scr[1 + TILE_BLOCKS * jt + u] for u in range(TILE_BLOCKS)], axis=1) > 0

    def far_max_body(jt):
        mask = far_mask(jt)
        for hc in range(n_att):
            _max_pass(far_tile(jt, hc), mask, hc * arows, mxf_scr)

    _for_each_tile_paired(n_far, far_max_body)

    ct_n = jnp.concatenate([ct_ref[0, i], ct_ref[0, i + 1]], axis=1)
    for hc in range(n_att):
        sc = _dot(qlat_scr[hc * arows:(hc + 1) * arows, :], ct_n)
        for r in range(hpc):
            h = hc * hpc + r
            hr = slice(h * qb, (h + 1) * qb)
            sn = jnp.where(mask_n, sc[r * qb:(r + 1) * qb] + bd_ref[h], NEG_INF)
            sn_scr[hr, :] = sn
            m_far = jnp.max(mxf_scr[hr, :], axis=1, keepdims=True) + far_ref[NSA_HEADS + h]
            m_row = jnp.maximum(m_far, jnp.max(sn, axis=1, keepdims=True))
            mb_scr[hr, :] = jnp.broadcast_to(m_row, (qb, LANES))

    def far_exp_body(jt):
        mask = far_mask(jt)
        cv = c_ref[0, pl.ds(pl.multiple_of(qb + KEY_TILE * jt, qb), KEY_TILE), :]
        for hc in range(n_att):
            shifts = [mb_scr[(hc * hpc + r) * qb:(hc * hpc + r + 1) * qb, :] - far_ref[NSA_HEADS + hc * hpc + r]
                      for r in range(hpc)]
            _exp_pass(far_tile(jt, hc), mask, shifts, cv, hc * arows, ls_scr, acc_scr)

    _for_each_tile_paired(n_far, far_exp_body)

    cv_n = c_ref[0, pl.ds(pl.multiple_of(t0, qb), 2 * qb), :]
    for hc in range(n_att):
        hrs = [slice((hc * hpc + r) * qb, (hc * hpc + r + 1) * qb) for r in range(hpc)]
        _exp_pass([sn_scr[hr, :] for hr in hrs], None, [mb_scr[hr, :] for hr in hrs], cv_n, hc * arows,
                  ls_scr, acc_scr)

    def normalised(hr):
        return acc_scr[hr, :] * (1.0 / jnp.sum(ls_scr[hr, :], axis=1, keepdims=True))

    for pr in range(DSA_HEADS // 2):
        ra = slice(2 * pr * qb, (2 * pr + 1) * qb)
        rb = slice((2 * pr + 1) * qb, (2 * pr + 2) * qb)
        o_lat = jnp.concatenate([normalised(ra), normalised(rb)], axis=1)
        o_ref[0, :, pr * 128:(pr + 1) * 128] = _dot(o_lat.astype(wuv_ref.dtype), wuv_ref[pr]).astype(o_ref.dtype)


def _dsa(proj3, qi_t, w_t, ki, ct, c_pad, wuk_bd, wuv_bd, bias_d, far):
    b, s, _ = proj3.shape
    nq = s // Q_BLOCK
    rows = DSA_HEADS * Q_BLOCK
    n_tiles = -(-nq // TILE_BLOCKS)
    const = dict(pipeline_mode=pl.Buffered(1))

    def whole(a):
        nd = a.ndim
        return pl.BlockSpec((1,) + a.shape[1:], lambda bi, i: (bi,) + (0,) * (nd - 1))

    return pl.pallas_call(
        _dsa_kernel,
        grid=(b, nq),
        in_specs=[
            pl.BlockSpec(memory_space=pltpu.SMEM),
            pl.BlockSpec((1, Q_BLOCK, DSA_HEADS * DSA_HEAD_DIM), lambda bi, i: (bi, i, COL_QD // 1024)),
            pl.BlockSpec((1, 1, IDX_DIM, rows), lambda bi, i: (bi, i, 0, 0)),
            pl.BlockSpec((1, 1, IDX_HEADS, Q_BLOCK), lambda bi, i: (bi, i, 0, 0)),
            whole(ki), whole(ct), whole(c_pad),
            pl.BlockSpec(wuk_bd.shape, lambda bi, i: (0, 0, 0), **const),
            pl.BlockSpec(wuv_bd.shape, lambda bi, i: (0, 0, 0), **const),
            pl.BlockSpec(bias_d.shape, lambda bi, i: (0, 0, 0), **const),
        ],
        out_specs=pl.BlockSpec((1, Q_BLOCK, DSA_HEADS * DSA_HEAD_DIM), lambda bi, i: (bi, i, 0)),
        out_shape=jax.ShapeDtypeStruct((b, s, DSA_HEADS * DSA_HEAD_DIM), MXU_DTYPE),
        scratch_shapes=[pltpu.VMEM((rows, DSA_KV_LATENT), MXU_DTYPE),
                        pltpu.VMEM((Q_BLOCK + KEY_TILE * n_tiles, Q_BLOCK), jnp.int32),
                        pltpu.VMEM((1 + TILE_BLOCKS * n_tiles, Q_BLOCK, Q_BLOCK), jnp.int32),
                        pltpu.VMEM((rows, LANES), jnp.float32),
                        pltpu.VMEM((rows, LANES), jnp.float32),
                        pltpu.VMEM((rows, LANES), jnp.float32),
                        pltpu.VMEM((rows, DSA_KV_LATENT), jnp.float32),
                        pltpu.VMEM((rows, 2 * Q_BLOCK), jnp.float32),
                        pltpu.VMEM((1, Q_BLOCK), jnp.int32)],
        compiler_params=pltpu.CompilerParams(
            dimension_semantics=("arbitrary", "arbitrary"), vmem_limit_bytes=VMEM_LIMIT_BYTES),
        name="dsa_attention",
    )(far, proj3, qi_t, w_t, ki, ct, c_pad, wuk_bd, wuv_bd, bias_d)


def _t5_bucket(dist):
    n = jnp.maximum(dist, 0)
    exact = REL_BUCKETS // 2
    nf = jnp.maximum(n, 1).astype(jnp.float32)
    large = exact + (jnp.log(nf / exact) / math.log(REL_MAX_DIST / exact)
                     * (REL_BUCKETS - exact)).astype(jnp.int32)
    large = jnp.minimum(large, REL_BUCKETS - 1)
    return jnp.where(n < exact, n, large)


def _bias_tables(rel_bias, nc):
    n_pos, n_neg = 1024, NSA_CMP_STRIDE * nc + 1024
    per_dist = rel_bias[_t5_bucket(jnp.arange(n_pos))]
    ext = jnp.concatenate([jnp.broadcast_to(per_dist[:1], (n_neg, per_dist.shape[1])), per_dist], axis=0)
    flipped = ext[::-1].T

    def skew(vec, rows):
        m = vec.shape[-1]
        tiled = jnp.tile(vec, (1,) * (vec.ndim - 1) + (rows,))
        return tiled[..., :rows * (m - 1)].reshape(vec.shape[:-1] + (rows, m - 1))

    def toeplitz(offset, count, heads):
        start = n_pos - 1 - offset - (Q_BLOCK - 1)
        seg = flipped[heads, start:start + count + Q_BLOCK]
        return skew(seg, Q_BLOCK)[..., Q_BLOCK - 1:Q_BLOCK - 1 + count]

    nsa, dsa = slice(0, NSA_HEADS), slice(NSA_HEADS, NSA_HEADS + DSA_HEADS)
    bias_w = toeplitz(NSA_WINDOW, Q_BLOCK + NSA_WINDOW, nsa)
    bias_d = toeplitz(Q_BLOCK, 2 * Q_BLOCK, dsa)
    st, qa = NSA_CMP_STRIDE, Q_BLOCK // NSA_CMP_STRIDE
    base = n_pos - 1 - 225 - (st - 1) - st * (qa - 1)
    seg = flipped[nsa, base:base + st * (nc + qa)].reshape(NSA_HEADS, nc + qa, st)
    per_b = jnp.transpose(seg[..., ::-1], (0, 2, 1))
    bias_c = skew(per_b, qa)[..., qa - 1:qa - 1 + nc]
    bias_c = jnp.transpose(bias_c, (0, 2, 1, 3)).reshape(NSA_HEADS, Q_BLOCK, nc)
    return bias_w, bias_c, bias_d, rel_bias[REL_BUCKETS - 1]


def _overlap_t(s):
    n_blk, nc = s // NSA_SEL_LEN, s // NSA_CMP_STRIDE
    cmp_start = np.arange(nc) * NSA_CMP_STRIDE
    sel_start = np.arange(n_blk) * NSA_SEL_LEN
    ov = (cmp_start[None, :] < sel_start[:, None] + NSA_SEL_LEN) & (cmp_start[None, :] + NSA_CMP_LEN > sel_start[:, None])
    return jnp.asarray(ov.astype(np.float32)).astype(MXU_DTYPE)


def _block_diag_pairs(w_uk, w_uv):
    hp = DSA_HEADS // 2
    dh, lat = DSA_HEAD_DIM, DSA_KV_LATENT
    uk = jnp.transpose(w_uk, (1, 2, 0)).reshape(hp, 2, dh, lat)
    uv = jnp.transpose(w_uv, (1, 0, 2)).reshape(hp, 2, lat, dh)
    z_k, z_v = jnp.zeros((hp, dh, lat), w_uk.dtype), jnp.zeros((hp, lat, dh), w_uv.dtype)
    wuk_bd = jnp.concatenate([jnp.concatenate([uk[:, 0], z_k], axis=2),
                              jnp.concatenate([z_k, uk[:, 1]], axis=2)], axis=1)
    wuv_bd = jnp.concatenate([jnp.concatenate([uv[:, 0], z_v], axis=2),
                              jnp.concatenate([z_v, uv[:, 1]], axis=2)], axis=1)
    return wuk_bd.astype(MXU_DTYPE), wuv_bd.astype(MXU_DTYPE)


def _mixers(proj, b, s, pe_k, pe_v, wk1, wk2, wv1, wv2, g_kv, w_uk, w_uv, tables, ovt):
    bias_w, bias_c, bias_d, far = tables
    p3 = proj.reshape(b, s, N_IN_PAD)
    nq, nblk = s // Q_BLOCK, s // Q_BLOCK
    g, dk = NSA_KV_GROUPS, NSA_HEAD_DIM

    def piece(off, width):
        return p3[:, :, off:off + width]

    def heads_major(a, heads, dim):
        return a.reshape(b, nq, Q_BLOCK, heads, dim).transpose(0, 1, 3, 2, 4).reshape(b, nq, heads * Q_BLOCK, dim)

    def chunked(a):
        return a.reshape(b, s // 16, 16, g, dk).transpose(0, 3, 1, 2, 4).reshape(b, g, s // 16, 16 * dk)

    def keys_t(a, pad_blocks):
        t = a.reshape(b, nblk, Q_BLOCK, a.shape[-1]).transpose(0, 1, 3, 2)
        return jnp.pad(t, ((0, 0), (pad_blocks, 0), (0, 0), (0, 0)))

    def rows_padded(a, pad):
        return jnp.pad(a, ((0, 0), (pad, 0), (0, 0)))

    kct = _compress(chunked(piece(COL_KC, 256)), pe_k, wk1, wk2, transposed=True).reshape(b, g * dk, s // 16)
    vcm = _compress(chunked(piece(COL_VC, 256)), pe_v, wv1, wv2, transposed=False)
    vcm = vcm.transpose(0, 2, 1, 3).reshape(b, s // 16, g * dk)
    o_n = _nsa(p3, heads_major(piece(COL_QN, 1024), NSA_HEADS, dk), kct, vcm,
               keys_t(piece(COL_KS, 256), 1), rows_padded(piece(COL_VS, 256), Q_BLOCK),
               keys_t(piece(COL_KW, 256), NSA_WINDOW // Q_BLOCK), rows_padded(piece(COL_VW, 256), NSA_WINDOW),
               bias_w, bias_c, ovt, far)

    ckv = _ckvnorm(proj, g_kv).reshape(b, s, DSA_KV_LATENT)
    ki = piece(COL_SMALL + 48, IDX_DIM)
    qi_t = piece(COL_QI, 512).reshape(b, nq, Q_BLOCK, IDX_HEADS, IDX_DIM).transpose(0, 1, 4, 3, 2)
    qi_t = qi_t.reshape(b, nq, IDX_DIM, IDX_HEADS * Q_BLOCK)
    w_t = piece(COL_SMALL + 80, IDX_HEADS).reshape(b, nq, Q_BLOCK, IDX_HEADS).transpose(0, 1, 3, 2)
    wuk_bd, wuv_bd = _block_diag_pairs(w_uk, w_uv)
    o_d = _dsa(p3, qi_t, w_t, ki, keys_t(ckv, 1), rows_padded(ckv, Q_BLOCK), wuk_bd, wuv_bd, bias_d, far)
    return o_n.reshape(b * s, -1), o_d.reshape(b * s, -1)


def _permute_w_in(w_in_l):
    d = w_in_l.shape[0]
    sizes = (1024, 256, 256, 256, 256, 256, 256, 48, 1024, 256, 512, 32, 16, 4096)
    offs = np.concatenate([[0], np.cumsum(sizes)])
    (q_n, kc, vc, ks, vs, kw, vw, g_n, q_d, ckv, qi, ki, wi, a_m) = [
        w_in_l[:, int(offs[k]):int(offs[k + 1])] for k in range(len(sizes))]
    small = jnp.concatenate([g_n, ki, wi, jnp.zeros((d, 32), w_in_l.dtype)], axis=1)
    out = jnp.concatenate([a_m, q_n, q_d, kc, vc, ks, vs, kw, vw, qi, ckv, small,
                           jnp.zeros((d, N_IN_PAD - 8576), w_in_l.dtype)], axis=1)
    return out.astype(MXU_DTYPE)


def kernel(x, c, w_ada, b_ada, g_norm, w_ffn_in, w_ffn_out, w_in, nsa_pe_k, nsa_pe_v, nsa_cmp_k1, nsa_cmp_k2,
           nsa_cmp_v1, nsa_cmp_v2, dsa_g_kv, dsa_w_uk, dsa_w_uv, w_up_nsa, w_up_dsa, w_out, rel_bias, g_final):
    B, S, D = x.shape
    N = B * S
    mod = _adaln(c, w_ada, b_ada)[:, :B].reshape(DEPTH, B, N_SUBLAYERS, 3, 1, D)
    tables = _bias_tables(rel_bias, S // NSA_CMP_STRIDE)
    ovt = _overlap_t(S)
    x2d = x.reshape(N, D)
    for l in range(DEPTH):
        m0, m1, m2 = mod[l, :, 0], mod[l, :, 1], mod[l, :, 2]
        x2d = _ffn(x2d, m0[:, 0], m0[:, 1], m0[:, 2], g_norm[l, 0],
                   w_ffn_in[l, 0].astype(MXU_DTYPE), w_ffn_out[l, 0].astype(MXU_DTYPE), g_final,
                   seq=S, final_norm=False)
        proj = _inproj(x2d, m1[:, 0], m1[:, 1], g_norm[l, 1], _permute_w_in(w_in[l]), seq=S)
        o_n, o_d = _mixers(proj, B, S, nsa_pe_k[l], nsa_pe_v[l], nsa_cmp_k1[l], nsa_cmp_k2[l], nsa_cmp_v1[l],
                           nsa_cmp_v2[l], dsa_g_kv[l], dsa_w_uk[l], dsa_w_uv[l], tables, ovt)
        x2d = _merge(x2d, m1[:, 2], o_n, o_d, proj, w_up_nsa[l].astype(MXU_DTYPE),
                     w_up_dsa[l].astype(MXU_DTYPE), w_out[l].astype(MXU_DTYPE), seq=S)
        x2d = _ffn(x2d, m2[:, 0], m2[:, 1], m2[:, 2], g_norm[l, 2],
                   w_ffn_in[l, 1].astype(MXU_DTYPE), w_ffn_out[l, 1].astype(MXU_DTYPE), g_final,
                   seq=S, final_norm=(l == DEPTH - 1))
    return x2d.reshape(B, S, D)
```
